```python
import math
import jax, jax.numpy as jnp
from jax import lax
import numpy as np

D_MODEL = 1024
BATCH = 16
SEQ = 2048
DEPTH = 2

N_Q_HEADS = 8
N_KV_HEADS = 2
HEAD_DIM = 64
GQA_GROUP = N_Q_HEADS // N_KV_HEADS
ATTN_WIDTH = N_Q_HEADS * HEAD_DIM
KV_WIDTH = N_KV_HEADS * HEAD_DIM
WINDOW = 128
BLOCK = 128
ROPE_DIM = HEAD_DIM // 4
ROPE_THETA = 500000.0

HYENA_WIDTH = 512
SHORT_CONV = 3
FILTER_EMB_DIM = 33
FILTER_HIDDEN = 64
DECAY_TARGET = 1e-2
FAST_DECAY_PCT = 0.3
SLOW_DECAY_PCT = 1.5
MAX_DECAY = math.log(DECAY_TARGET) / FAST_DECAY_PCT
MIN_DECAY = math.log(DECAY_TARGET) / SLOW_DECAY_PCT

N_BRANCHES = 2
IN_WIDTH = ATTN_WIDTH + 2 * KV_WIDTH + 3 * HYENA_WIDTH + N_BRANCHES * D_MODEL

N_KEYS = 128
N_EXPERTS = N_KEYS * N_KEYS
PEER_HEADS = 8
PEER_TOPK = 16
QUERY_DIM = 256
HALF_QUERY = QUERY_DIM // 2
PEER_TOKEN_BLOCK = 128

EPS = 1e-6

kernel_name = "hybrid_swa_hyena_peer_encoder"


def rmsnorm(x, g):
    xf = x.astype(jnp.float32)
    r = lax.rsqrt(jnp.mean(xf * xf, axis=-1, keepdims=True) + EPS)
    return (xf * r * g.astype(jnp.float32)).astype(x.dtype)


def rope_tables(positions):
    inv_freq = ROPE_THETA ** (-jnp.arange(0, ROPE_DIM, 2, dtype=jnp.float32) / ROPE_DIM)
    ang = positions.astype(jnp.float32)[..., None] * inv_freq
    return jnp.cos(ang)[:, :, None, :], jnp.sin(ang)[:, :, None, :]


def apply_partial_rope(t, cos, sin):
    half = ROPE_DIM // 2
    t1 = t[..., :half].astype(jnp.float32)
    t2 = t[..., half:ROPE_DIM].astype(jnp.float32)
    rot = jnp.concatenate([t1 * cos - t2 * sin, t2 * cos + t1 * sin], axis=-1).astype(t.dtype)
    return jnp.concatenate([rot, t[..., ROPE_DIM:]], axis=-1)


def window_attention(q, k, v, sink):
    B, S = q.shape[0], q.shape[1]
    nb = S // BLOCK
    qb = q.reshape(B, nb, BLOCK, N_KV_HEADS, GQA_GROUP, HEAD_DIM)

    def band(t):
        tp = jnp.pad(t, ((0, 0), (BLOCK, BLOCK), (0, 0), (0, 0)))
        tp = tp.reshape(B, nb + 2, BLOCK, N_KV_HEADS, HEAD_DIM)
        return jnp.concatenate([tp[:, :-2], tp[:, 1:-1], tp[:, 2:]], axis=2)

    kb, vb = band(k), band(v)
    scale = HEAD_DIM ** -0.5
    s = jnp.einsum('bnqhgd,bnkhd->bnhgqk', qb, kb).astype(jnp.float32) * scale
    blk = jnp.arange(nb)[:, None]
    q_pos = blk * BLOCK + jnp.arange(BLOCK)[None, :]
    k_pos = (blk - 1) * BLOCK + jnp.arange(3 * BLOCK)[None, :]
    valid = (jnp.abs(q_pos[:, :, None] - k_pos[:, None, :]) <= WINDOW) \
        & (k_pos[:, None, :] >= 0) & (k_pos[:, None, :] < S)
    s = jnp.where(valid[None, :, None, None], s, -1e30)
    sink_l = sink.astype(jnp.float32).reshape(N_KV_HEADS, GQA_GROUP)[None, None, :, :, None, None]
    m = jnp.maximum(jnp.max(s, axis=-1, keepdims=True), sink_l)
    p = jnp.exp(s - m)
    denom = jnp.sum(p, axis=-1, keepdims=True) + jnp.exp(sink_l - m)
    out = jnp.einsum('bnhgqk,bnkhd->bnqhgd', (p / denom).astype(v.dtype), vb)
    return out.reshape(B, S, ATTN_WIDTH)


def hyena_filter(L, w1, b1, fr1, w2, b2, fr2, w3, b3):
    t = jnp.linspace(0.0, 1.0, L, dtype=jnp.float32)[:, None]
    bands = (FILTER_EMB_DIM - 1) // 2
    w = 2.0 * math.pi * jnp.arange(L, dtype=jnp.float32)[:, None] / L
    f = jnp.linspace(1e-4, bands - 1, bands, dtype=jnp.float32)[None, :]
    z = jnp.concatenate([t, jnp.cos(f * w), -jnp.sin(f * w)], axis=-1)
    h = jnp.sin(fr1 * (z @ w1 + b1))
    h = jnp.sin(fr2 * (h @ w2 + b2))
    h = (h @ w3 + b3).astype(jnp.float32)
    deltas = jnp.linspace(MIN_DECAY, MAX_DECAY, HYENA_WIDTH, dtype=jnp.float32)
    decay = jnp.exp(-t * jnp.abs(deltas)[None, :])
    h = h.reshape(L, 2, HYENA_WIDTH) * decay[:, None, :]
    h_f, h_b = h[:, 0], h[:, 1]
    return jnp.concatenate([h_f[:1] + h_b[:1], h_f[1:],
                            jnp.zeros((1, HYENA_WIDTH), jnp.float32), h_b[:0:-1]], axis=0)


def hyena_mixer(u, conv_w, conv_b, w1, b1, fr1, w2, b2, fr2, w3, b3, skip):
    B, L = u.shape[0], u.shape[1]
    c3 = 3 * HYENA_WIDTH
    uc = lax.conv_general_dilated(u, conv_w.reshape(SHORT_CONV, 1, c3).astype(u.dtype),
                                  window_strides=(1,), padding=((1, 1),),
                                  dimension_numbers=('NWC', 'WIO', 'NWC'),
                                  feature_group_count=c3) + conv_b
    x0, x1, vv = jnp.split(uc, 3, axis=-1)
    zin = (x1 * vv).astype(jnp.float32)
    kern = hyena_filter(L, w1, b1, fr1, w2, b2, fr2, w3, b3)
    zf = jnp.fft.rfft(zin, n=2 * L, axis=1)
    kf = jnp.fft.rfft(kern, n=2 * L, axis=0)
    y = jnp.fft.irfft(zf * kf[None], n=2 * L, axis=1)[:, :L]
    y = y + zin * skip.astype(jnp.float32)
    return (x0.astype(jnp.float32) * y).astype(u.dtype)


def peer_ffn(xn, w_query, sub_keys, expert_u, expert_v):
    B, S, D = xn.shape
    xt = xn.reshape(B * S // PEER_TOKEN_BLOCK, PEER_TOKEN_BLOCK, D)

    def block(xb):
        c = xb.shape[0]
        q = (xb @ w_query).reshape(c, PEER_HEADS, 2, HALF_QUERY)
        s = jnp.einsum('chpd,pkd->chpk', q, sub_keys)
        s_top, i_top = lax.top_k(s, PEER_TOPK)
        cand = s_top[:, :, 0, :, None] + s_top[:, :, 1, None, :]
        cand_idx = i_top[:, :, 0, :, None] * N_KEYS + i_top[:, :, 1, None, :]
        best, pos = lax.top_k(cand.reshape(c, PEER_HEADS, PEER_TOPK * PEER_TOPK), PEER_TOPK)
        idx = jnp.take_along_axis(cand_idx.reshape(c, PEER_HEADS, PEER_TOPK * PEER_TOPK), pos, axis=-1)
        g = jax.nn.softmax(best.astype(jnp.float32), axis=-1).astype(xb.dtype)
        u = jnp.take(expert_u, idx, axis=0)
        act = jax.nn.gelu(jnp.einsum('chkd,cd->chk', u, xb), approximate=False)
        vsel = jnp.take(expert_v, idx, axis=0)
        return jnp.einsum('chk,chkd->cd', g * act, vsel)

    return lax.map(block, xt).reshape(B, S, D)


def setup_inputs(seed: int = 0) -> dict:
    key = jax.random.key(seed)
    ks = jax.random.split(key, 24)
    f32 = jnp.float32
    nrm = lambda k, shape, scale: jax.random.normal(k, shape, f32) * scale
    C = HYENA_WIDTH
    return {
        "x": nrm(ks[0], (BATCH, SEQ, D_MODEL), 1.0),
        "positions": jnp.broadcast_to(jnp.arange(SEQ, dtype=jnp.int32), (BATCH, SEQ)),
        "norm_mix": 1.0 + nrm(ks[1], (DEPTH, D_MODEL), 0.02),
        "w_in": nrm(ks[2], (DEPTH, D_MODEL, IN_WIDTH), D_MODEL ** -0.5),
        "conv_w": nrm(ks[3], (DEPTH, SHORT_CONV, 3 * C), SHORT_CONV ** -0.5),
        "conv_b": nrm(ks[4], (DEPTH, 3 * C), 0.02),
        "filt_w1": nrm(ks[5], (DEPTH, FILTER_EMB_DIM, FILTER_HIDDEN), FILTER_EMB_DIM ** -0.5),
        "filt_b1": nrm(ks[6], (DEPTH, FILTER_HIDDEN), 0.1),
        "filt_freq1": 1.0 + nrm(ks[7], (DEPTH, FILTER_HIDDEN), 0.1),
        "filt_w2": nrm(ks[8], (DEPTH, FILTER_HIDDEN, FILTER_HIDDEN), FILTER_HIDDEN ** -0.5),
        "filt_b2": nrm(ks[9], (DEPTH, FILTER_HIDDEN), 0.1),
        "filt_freq2": 1.0 + nrm(ks[10], (DEPTH, FILTER_HIDDEN), 0.1),
        "filt_w3": nrm(ks[11], (DEPTH, FILTER_HIDDEN, 2 * C), 0.02 * FILTER_HIDDEN ** -0.5),
        "filt_b3": nrm(ks[12], (DEPTH, 2 * C), 0.002),
        "hyena_skip": nrm(ks[13], (DEPTH, C), 1.0),
        "attn_sink": nrm(ks[14], (DEPTH, N_Q_HEADS), 0.5),
        "w_attn_branch": nrm(ks[15], (DEPTH, ATTN_WIDTH, D_MODEL), ATTN_WIDTH ** -0.5),
        "w_hyena_branch": nrm(ks[16], (DEPTH, C, D_MODEL), C ** -0.5),
        "w_out": nrm(ks[17], (DEPTH, D_MODEL, D_MODEL), D_MODEL ** -0.5),
        "norm_ffn": 1.0 + nrm(ks[18], (DEPTH, D_MODEL), 0.02),
        "w_query": nrm(ks[19], (DEPTH, D_MODEL, PEER_HEADS * QUERY_DIM), D_MODEL ** -0.5),
        "sub_keys": nrm(ks[20], (DEPTH, 2, N_KEYS, HALF_QUERY), HALF_QUERY ** -0.5),
        "expert_u": nrm(ks[21], (DEPTH, N_EXPERTS, D_MODEL), D_MODEL ** -0.5),
        "expert_v": nrm(ks[22], (DEPTH, N_EXPERTS, D_MODEL), PEER_HEADS ** -0.5),
        "norm_final": 1.0 + nrm(ks[23], (D_MODEL,), 0.02),
    }


def reference(x, positions, norm_mix, w_in, conv_w, conv_b, filt_w1, filt_b1, filt_freq1,
              filt_w2, filt_b2, filt_freq2, filt_w3, filt_b3, hyena_skip, attn_sink,
              w_attn_branch, w_hyena_branch, w_out, norm_ffn, w_query, sub_keys,
              expert_u, expert_v, norm_final):
    B, S = x.shape[0], x.shape[1]
    cos, sin = rope_tables(positions)
    splits = [ATTN_WIDTH, ATTN_WIDTH + KV_WIDTH, ATTN_WIDTH + 2 * KV_WIDTH,
              ATTN_WIDTH + 2 * KV_WIDTH + 3 * HYENA_WIDTH,
              ATTN_WIDTH + 2 * KV_WIDTH + 3 * HYENA_WIDTH + D_MODEL]
    for l in range(DEPTH):
        xn = rmsnorm(x, norm_mix[l])
        proj = xn @ w_in[l]
        q, k, v, hy, g_a, g_h = jnp.split(proj, splits, axis=-1)
        q = apply_partial_rope(q.reshape(B, S, N_Q_HEADS, HEAD_DIM), cos, sin)
        k = apply_partial_rope(k.reshape(B, S, N_KV_HEADS, HEAD_DIM), cos, sin)
        v = v.reshape(B, S, N_KV_HEADS, HEAD_DIM)
        y_a = window_attention(q, k, v, attn_sink[l]) @ w_attn_branch[l]
        y_h = hyena_mixer(hy, conv_w[l], conv_b[l], filt_w1[l], filt_b1[l], filt_freq1[l],
                          filt_w2[l], filt_b2[l], filt_freq2[l], filt_w3[l], filt_b3[l],
                          hyena_skip[l]) @ w_hyena_branch[l]
        merged = jax.nn.sigmoid(g_a) * y_a + jax.nn.sigmoid(g_h) * y_h
        x = x + merged @ w_out[l]
        x = x + peer_ffn(rmsnorm(x, norm_ffn[l]), w_query[l], sub_keys[l], expert_u[l], expert_v[l])
    return rmsnorm(x, norm_final)
```

```python
import functools
import math

import jax
import jax.numpy as jnp
from jax import lax
from jax.experimental import pallas as pl
from jax.experimental.pallas import tpu as pltpu

F32 = jnp.float32
BF16 = jnp.bfloat16

D_MODEL = 1024
N_Q_HEADS = 8
N_KV_HEADS = 2
HEAD_DIM = 64
GQA_GROUP = N_Q_HEADS // N_KV_HEADS
ATTN_WIDTH = N_Q_HEADS * HEAD_DIM
KV_WIDTH = N_KV_HEADS * HEAD_DIM
WINDOW = 128
BLOCK = 128
ROPE_DIM = HEAD_DIM // 4
ROPE_THETA = 500000.0
HYENA_WIDTH = 512
FILTER_EMB_DIM = 33
FILTER_HIDDEN = 64
DECAY_TARGET = 1e-2
MAX_DECAY = math.log(DECAY_TARGET) / 0.3
MIN_DECAY = math.log(DECAY_TARGET) / 1.5
N_KEYS = 128
N_EXPERTS = N_KEYS * N_KEYS
PEER_HEADS = 8
PEER_TOPK = 16
HALF_QUERY = 128
N_SLOTS = PEER_HEADS * PEER_TOPK
EPS = 1e-6

LANES = 128
SUBLANES = 8
VMEM_LIMIT_BYTES = 56 * 1024 * 1024

GATE_OFF = 0
HY_OFF = 2 * D_MODEL
Q_OFF = HY_OFF + 3 * HYENA_WIDTH
K_OFF = Q_OFF + ATTN_WIDTH
V_OFF = K_OFF + KV_WIDTH
IN_WIDTH = V_OFF + KV_WIDTH
ROPE_WIDTH = ATTN_WIDTH + KV_WIDTH

ROW_TILE = 256
FREQ_TILE = 512
TOPK_TILE = 256
GATHER_TOKENS = 64
HALF_D = D_MODEL // 2
PACK_ROWS = HALF_D // LANES


def _cparams(*sem):
    return pltpu.CompilerParams(dimension_semantics=sem, vmem_limit_bytes=VMEM_LIMIT_BYTES)


def _rms(x, g):
    r = lax.rsqrt(jnp.mean(x * x, axis=-1, keepdims=True) + EPS)
    return x * r * g


def _in_proj_kernel(*refs, has_add):
    if has_add:
        h_ref, p_ref, g_ref, w_ref, c_ref, s1_ref, s2_ref, x_out, proj_out = refs
        x = h_ref[...] + p_ref[...]
        x_out[...] = x
    else:
        h_ref, g_ref, w_ref, c_ref, s1_ref, s2_ref, proj_out = refs
        x = h_ref[...]
    xn = _rms(x, g_ref[...]).astype(BF16)
    proj_out[:, :Q_OFF] = jnp.dot(xn, w_ref[:, :Q_OFF], preferred_element_type=F32)
    qk = jnp.dot(xn, w_ref[:, Q_OFF:V_OFF], preferred_element_type=F32)
    reps = ROPE_WIDTH // LANES
    cos = jnp.concatenate([c_ref[...]] * reps, axis=1)
    s1 = jnp.concatenate([s1_ref[...]] * reps, axis=1)
    s2 = jnp.concatenate([s2_ref[...]] * reps, axis=1)
    half = ROPE_DIM // 2
    rot = qk * cos + pltpu.roll(qk, ROPE_WIDTH - half, axis=1) * s1 + pltpu.roll(qk, half, axis=1) * s2
    proj_out[:, Q_OFF:V_OFF] = rot
    proj_out[:, V_OFF:] = jnp.dot(xn, w_ref[:, V_OFF:], preferred_element_type=F32)


def _in_proj(h, p, g, w, cos_t, s1_t, s2_t):
    T = h.shape[0]
    tm = min(ROW_TILE, T)
    has_add = p is not None
    row = lambda i: (i, 0)
    fixed = lambda i: (0, 0)
    x_spec = pl.BlockSpec((tm, D_MODEL), row)
    tab_spec = pl.BlockSpec((tm, LANES), row)
    in_specs = [x_spec] + ([x_spec] if has_add else []) + [
        pl.BlockSpec((1, D_MODEL), fixed), pl.BlockSpec((D_MODEL, IN_WIDTH), fixed), tab_spec, tab_spec, tab_spec]
    proj_shape = jax.ShapeDtypeStruct((T, IN_WIDTH), F32)
    proj_spec = pl.BlockSpec((tm, IN_WIDTH), row)
    if has_add:
        out_shape, out_specs = (jax.ShapeDtypeStruct((T, D_MODEL), F32), proj_shape), (x_spec, proj_spec)
        args = (h, p, g, w, cos_t, s1_t, s2_t)
    else:
        out_shape, out_specs = proj_shape, proj_spec
        args = (h, g, w, cos_t, s1_t, s2_t)
    return pl.pallas_call(
        functools.partial(_in_proj_kernel, has_add=has_add), grid=(T // tm,), in_specs=in_specs,
        out_specs=out_specs, out_shape=out_shape, compiler_params=_cparams("parallel"), name="in_proj")(*args)


def _attn_kernel(sink_ref, q_ref, kp_ref, ko_ref, kn_ref, o_ref, *, nb):
    i = pl.program_id(1)
    q = q_ref[...]
    kv = jnp.concatenate([kp_ref[...], ko_ref[...], kn_ref[...]], axis=0)
    r = lax.broadcasted_iota(jnp.int32, (BLOCK, 3 * BLOCK), 0)
    c = lax.broadcasted_iota(jnp.int32, (BLOCK, 3 * BLOCK), 1)
    d = c - r
    valid = (d >= BLOCK - WINDOW) & (d <= BLOCK + WINDOW)
    valid &= (c >= BLOCK) | (i > 0)
    valid &= (c < 2 * BLOCK) | (i < nb - 1)
    valid = jnp.concatenate([valid] * GQA_GROUP, axis=0)
    scale = HEAD_DIM ** -0.5
    for g in range(N_KV_HEADS):
        k = kv[:, g * HEAD_DIM:(g + 1) * HEAD_DIM].astype(BF16)
        v = kv[:, KV_WIDTH + g * HEAD_DIM:KV_WIDTH + (g + 1) * HEAD_DIM].astype(BF16)
        heads = [g * GQA_GROUP + hh for hh in range(GQA_GROUP)]
        qs = jnp.concatenate([q[:, h * HEAD_DIM:(h + 1) * HEAD_DIM] for h in heads], axis=0).astype(BF16)
        s = lax.dot_general(qs, k, (((1,), (1,)), ((), ())), preferred_element_type=F32) * scale
        s = jnp.where(valid, s, -1e30)
        sink = jnp.concatenate([jnp.full((BLOCK, 1), sink_ref[h], F32) for h in heads], axis=0)
        m = jnp.maximum(jnp.max(s, axis=-1, keepdims=True), sink)
        p = jnp.exp(s - m)
        denom = jnp.sum(p, axis=-1, keepdims=True) + jnp.exp(sink - m)
        o = jnp.dot(p.astype(BF16), v, preferred_element_type=F32) / denom
        for hh, h in enumerate(heads):
            o_ref[:, h * HEAD_DIM:(h + 1) * HEAD_DIM] = o[hh * BLOCK:(hh + 1) * BLOCK].astype(o_ref.dtype)


def _attention(proj, sink, B, S):
    nb = S // BLOCK
    qcol = Q_OFF // ATTN_WIDTH
    kvcol = K_OFF // (2 * KV_WIDTH)
    kv_spec = lambda f: pl.BlockSpec((BLOCK, 2 * KV_WIDTH), f)
    return pl.pallas_call(
        functools.partial(_attn_kernel, nb=nb), grid=(B, nb),
        in_specs=[pl.BlockSpec(memory_space=pltpu.SMEM),
                  pl.BlockSpec((BLOCK, ATTN_WIDTH), lambda b, i: (b * nb + i, qcol)),
                  kv_spec(lambda b, i: (b * nb + jnp.maximum(i - 1, 0), kvcol)),
                  kv_spec(lambda b, i: (b * nb + i, kvcol)),
                  kv_spec(lambda b, i: (b * nb + jnp.minimum(i + 1, nb - 1), kvcol))],
        out_specs=pl.BlockSpec((BLOCK, ATTN_WIDTH), lambda b, i: (b * nb + i, 0)),
        out_shape=jax.ShapeDtypeStruct((B * S, ATTN_WIDTH), BF16),
        compiler_params=_cparams("parallel", "parallel"), name="window_attn")(sink, proj, proj, proj, proj)


HY_COLS = 256


def _short_conv(u, w, b):
    L = u.shape[0]
    row = lax.broadcasted_iota(jnp.int32, u.shape, 0)
    prev = jnp.where(row == 0, 0.0, pltpu.roll(u, 1, axis=0))
    nxt = jnp.where(row == L - 1, 0.0, pltpu.roll(u, L - 1, axis=0))
    return w[0:1] * prev + w[1:2] * u + w[2:3] * nxt + b


def _hyena_prep_kernel(u0, u1, u2, w0, w1, w2, b0, b1, b2, x0_out, z_out, zb_out):
    x0_out[...] = _short_conv(u0[...], w0[...], b0[...])
    z = _short_conv(u1[...], w1[...], b1[...]) * _short_conv(u2[...], w2[...], b2[...])
    z_out[...] = z
    zb_out[...] = z.astype(BF16)


def _hyena_prep(proj, conv_w, conv_b, B, L):
    nc = HYENA_WIDTH // HY_COLS
    base = HY_OFF // HY_COLS
    u_spec = lambda part: pl.BlockSpec((L, HY_COLS), lambda b, c: (b, base + part * nc + c))
    w_spec = lambda part: pl.BlockSpec((3, HY_COLS), lambda b, c: (0, part * nc + c))
    b_spec = lambda part: pl.BlockSpec((1, HY_COLS), lambda b, c: (0, part * nc + c))
    o_spec = pl.BlockSpec((L, HY_COLS), lambda b, c: (b, c))
    shp = lambda dt: jax.ShapeDtypeStruct((B * L, HYENA_WIDTH), dt)
    return pl.pallas_call(
        _hyena_prep_kernel, grid=(B, nc),
        in_specs=[u_spec(0), u_spec(1), u_spec(2), w_spec(0), w_spec(1), w_spec(2), b_spec(0), b_spec(1), b_spec(2)],
        out_specs=(o_spec, o_spec, o_spec), out_shape=(shp(F32), shp(F32), shp(BF16)),
        compiler_params=_cparams("parallel", "parallel"), name="hyena_prep",
    )(proj, proj, proj, conv_w, conv_w, conv_w, conv_b, conv_b, conv_b)


def _filter_mlp_kernel(z_ref, w1, b1, f1, w2, b2, f2, w3, b3, dec_ref, o_ref):
    h = jnp.sin(f1[...] * (jnp.dot(z_ref[...], w1[...], preferred_element_type=F32) + b1[...]))
    h = jnp.sin(f2[...] * (jnp.dot(h, w2[...], preferred_element_type=F32) + b2[...]))
    h = jnp.dot(h, w3[...], preferred_element_type=F32) + b3[...]
    dec = dec_ref[...]
    o_ref[...] = h * jnp.concatenate([dec, dec], axis=1)


def _filter_mlp(z, w1, b1, f1, w2, b2, f2, w3, b3, decay):
    L = z.shape[0]
    tl = min(512, L)
    fixed = lambda a: pl.BlockSpec(a.shape, lambda i: (0, 0))
    return pl.pallas_call(
        _filter_mlp_kernel, grid=(L // tl,),
        in_specs=[pl.BlockSpec((tl, LANES), lambda i: (i, 0))] + [fixed(a) for a in (w1, b1, f1, w2, b2, f2, w3, b3)]
        + [pl.BlockSpec((tl, HYENA_WIDTH), lambda i: (i, 0))],
        out_specs=pl.BlockSpec((tl, 2 * HYENA_WIDTH), lambda i: (i, 0)),
        out_shape=jax.ShapeDtypeStruct((L, 2 * HYENA_WIDTH), F32),
        compiler_params=_cparams("parallel"), name="filter_mlp")(z, w1, b1, f1, w2, b2, f2, w3, b3, decay)


def _split_bf16(a):
    hi = a.astype(BF16)
    return hi, (a - hi.astype(F32)).astype(BF16)


def _filter_dft_kernel(fre, fim, k0_ref, k1_ref, o_re, o_im, *, n_fft):
    ft = fre.shape[0]
    k = pl.program_id(0) * ft + lax.broadcasted_iota(jnp.int32, (ft, 1), 0)
    sign = (1 - 2 * (k & 1)).astype(F32)
    amp = jnp.where(k == 0, 1.0 / n_fft, 2.0 / n_fft)
    k0h, k0l = _split_bf16(k0_ref[...])
    k1h, k1l = _split_bf16(k1_ref[...])

    def part(f):
        d = lambda a: jnp.dot(f, a, preferred_element_type=F32)
        return amp * ((d(k0h) + d(k0l)) + sign * (d(k1h) + d(k1l)))

    o_re[...] = part(fre[...])
    o_im[...] = part(fim[...])


def _filter_dft(fmat, k0, k1):
    L = k0.shape[0]
    ft = min(FREQ_TILE, L)
    nf = L // ft
    half = pl.BlockSpec((L, HYENA_WIDTH), lambda f: (0, 0))
    o_spec = pl.BlockSpec((ft, HYENA_WIDTH), lambda f: (f, 0))
    shp = jax.ShapeDtypeStruct((L, HYENA_WIDTH), F32)
    return pl.pallas_call(
        functools.partial(_filter_dft_kernel, n_fft=2 * L), grid=(nf,),
        in_specs=[pl.BlockSpec((ft, L), lambda f: (f, 0)), pl.BlockSpec((ft, L), lambda f: (nf + f, 0)), half, half],
        out_specs=(o_spec, o_spec), out_shape=(shp, shp),
        compiler_params=_cparams("parallel"), name="filter_dft")(fmat, fmat, k0, k1)


def _long_conv_kernel(z_ref, fre, fim, gre, gim, kre_ref, kim_ref, y_ref):
    f = pl.program_id(1)
    z = z_ref[...]
    zre = jnp.dot(fre[...], z, preferred_element_type=F32)
    zim = jnp.dot(fim[...], z, preferred_element_type=F32)
    kre, kim = kre_ref[...], kim_ref[...]
    a, b, c, d = zre * kre, zim * kim, zre * kim, zim * kre
    real_row = (lax.broadcasted_iota(jnp.int32, zre.shape, 0) == 0) & (f == 0)
    pre = jnp.where(real_row, a, a - b).astype(BF16)
    pim = jnp.where(real_row, b, c + d).astype(BF16)
    y = jnp.dot(gre[...], pre, preferred_element_type=F32) + jnp.dot(gim[...], pim, preferred_element_type=F32)

    @pl.when(f == 0)
    def _():
        y_ref[...] = y

    @pl.when(f > 0)
    def _():
        y_ref[...] += y


def _long_conv(zb, fmat, gmat, kre, kim, B, L):
    ft = min(FREQ_TILE, L)
    nf = L // ft
    k_spec = pl.BlockSpec((ft, HYENA_WIDTH), lambda b, f: (f, 0))
    return pl.pallas_call(
        _long_conv_kernel, grid=(B, nf),
        in_specs=[pl.BlockSpec((L, HYENA_WIDTH), lambda b, f: (b, 0)),
                  pl.BlockSpec((ft, L), lambda b, f: (f, 0)), pl.BlockSpec((ft, L), lambda b, f: (nf + f, 0)),
                  pl.BlockSpec((L, ft), lambda b, f: (0, f)), pl.BlockSpec((L, ft), lambda b, f: (0, nf + f)),
                  k_spec, k_spec],
        out_specs=pl.BlockSpec((L, HYENA_WIDTH), lambda b, f: (b, 0)),
        out_shape=jax.ShapeDtypeStruct((B * L, HYENA_WIDTH), F32),
        compiler_params=_cparams("parallel", "arbitrary"), name="long_conv")(zb, fmat, fmat, gmat, gmat, kre, kim)


def _dft_matrices(L):
    n = 2 * L

    def build(k, t):
        ang = ((k * t) % n).astype(F32) * (2.0 * math.pi / n)
        re = jnp.cos(ang)
        im = jnp.where(k == 0, (1 - 2 * (t & 1)).astype(F32), -jnp.sin(ang))
        return re.astype(BF16), im.astype(BF16)

    col = jnp.arange(L, dtype=jnp.int32)[:, None]
    row = jnp.arange(L, dtype=jnp.int32)[None, :]
    fre, fim = build(col, row)
    gre, gim = build(row, col)
    return jnp.concatenate([fre, fim], axis=0), jnp.concatenate([gre, gim], axis=1)


def _filter_inputs(L):
    t = jnp.linspace(0.0, 1.0, L, dtype=F32)[:, None]
    bands = (FILTER_EMB_DIM - 1) // 2
    w = 2.0 * math.pi * jnp.arange(L, dtype=F32)[:, None] / L
    f = jnp.linspace(1e-4, bands - 1, bands, dtype=F32)[None, :]
    z = jnp.concatenate([t, jnp.cos(f * w), -jnp.sin(f * w)], axis=-1)
    deltas = jnp.linspace(MIN_DECAY, MAX_DECAY, HYENA_WIDTH, dtype=F32)
    decay = jnp.exp(-t * jnp.abs(deltas)[None, :])
    return jnp.pad(z, ((0, 0), (0, LANES - FILTER_EMB_DIM))), decay


def _pad_to(a, rows, cols):
    return jnp.pad(a, ((0, rows - a.shape[0]), (0, cols - a.shape[1])))


def _hyena_filter_spectrum(L, z_emb, decay, fmat, w1, b1, fr1, w2, b2, fr2, w3, b3):
    h = _filter_mlp(z_emb, _pad_to(w1, LANES, LANES), _pad_to(b1[None], 1, LANES), _pad_to(fr1[None], 1, LANES),
                    _pad_to(w2, LANES, LANES), _pad_to(b2[None], 1, LANES), _pad_to(fr2[None], 1, LANES),
                    _pad_to(w3, LANES, 2 * HYENA_WIDTH), b3[None], decay)
    h_f, h_b = h[:, :HYENA_WIDTH], h[:, HYENA_WIDTH:]
    k0 = jnp.concatenate([h_f[:1] + h_b[:1], h_f[1:]], axis=0)
    k1 = jnp.concatenate([jnp.zeros((1, HYENA_WIDTH), F32), h_b[:0:-1]], axis=0)
    return _filter_dft(fmat, k0, k1)


def _merge_kernel(x_ref, a_ref, x0_ref, z_ref, y_ref, g_ref, skip_ref, wab, whb, wout, nf_ref, wq, h_out, hn_out,
                  q_out):
    ya = jnp.dot(a_ref[...], wab[...], preferred_element_type=F32)
    hy = x0_ref[...] * (y_ref[...] + z_ref[...] * skip_ref[...])
    yh = jnp.dot(hy.astype(BF16), whb[...], preferred_element_type=F32)
    g = g_ref[...]
    merged = jax.nn.sigmoid(g[:, :D_MODEL]) * ya + jax.nn.sigmoid(g[:, D_MODEL:]) * yh
    h = x_ref[...] + jnp.dot(merged.astype(BF16), wout[...], preferred_element_type=F32)
    h_out[...] = h
    hn = _rms(h, nf_ref[...])
    hn_out[...] = hn
    q_out[...] = jnp.dot(hn.astype(BF16), wq[...], preferred_element_type=F32)


def _merge(x, attn, x0, z, y, proj, skip, wab, whb, wout, nf, wq):
    T = x.shape[0]
    tm = min(ROW_TILE, T)
    row = lambda w: pl.BlockSpec((tm, w), lambda i: (i, 0))
    fixed = lambda a: pl.BlockSpec(a.shape, lambda i: (0, 0))
    qw = wq.shape[1]
    return pl.pallas_call(
        _merge_kernel, grid=(T // tm,),
        in_specs=[row(D_MODEL), row(ATTN_WIDTH), row(HYENA_WIDTH), row(HYENA_WIDTH), row(HYENA_WIDTH),
                  row(2 * D_MODEL), fixed(skip), fixed(wab), fixed(whb), fixed(wout), fixed(nf), fixed(wq)],
        out_specs=(row(D_MODEL), row(D_MODEL), row(qw)),
        out_shape=(jax.ShapeDtypeStruct((T, D_MODEL), F32), jax.ShapeDtypeStruct((T, D_MODEL), F32),
                   jax.ShapeDtypeStruct((T, qw), F32)),
        compiler_params=_cparams("parallel"), name="merge_query")(x, attn, x0, z, y, proj, skip, wab, whb, wout, nf, wq)


NEG_INF = float("-inf")


def _top_rows(s, pos, k):
    big = jnp.int32(2 ** 30)
    vals, sel = [], []
    for _ in range(k):
        m = jnp.max(s, axis=0, keepdims=True)
        p = jnp.min(jnp.where(s == m, pos, big), axis=0, keepdims=True)
        vals.append(m)
        sel.append(p)
        s = jnp.where(pos == p, NEG_INF, s)
    return vals, sel


def _peer_topk_kernel(q_ref, keys_ref, idx_out, gate_out):
    c = q_ref.shape[0]
    kk = PEER_TOPK
    key_pos = lax.broadcasted_iota(jnp.int32, (N_KEYS, c), 0)
    a_small = kk // 2
    b_iota = lax.broadcasted_iota(jnp.int32, (a_small, c), 0)
    idx_rows, gate_rows = [], []
    for h in range(PEER_HEADS):
        tops = []
        for p in range(2):
            qhp = q_ref[:, (2 * h + p) * HALF_QUERY:(2 * h + p + 1) * HALF_QUERY].astype(BF16)
            s = lax.dot_general(keys_ref[p], qhp, (((1,), (1,)), ((), ())), preferred_element_type=F32)
            vals, sel = _top_rows(s, key_pos, kk)
            tops.append((jnp.concatenate(vals, axis=0), jnp.concatenate(sel, axis=0)))
        (s0, i0), (s1, i1) = tops
        cand = [s0[0:1] + s1]
        eidx = [i0[0:1] * N_KEYS + i1]
        cpos = [lax.broadcasted_iota(jnp.int32, (kk, c), 0)]
        for a in range(1, a_small):
            keep = (a + 1) * (b_iota + 1) <= kk
            cand.append(jnp.where(keep, s0[a:a + 1] + s1[:a_small], NEG_INF))
            eidx.append(i0[a:a + 1] * N_KEYS + i1[:a_small])
            cpos.append(a * kk + b_iota)
        cand.append(s0[a_small:] + s1[0:1])
        eidx.append(i0[a_small:] * N_KEYS + i1[0:1])
        cpos.append((b_iota + a_small) * kk)
        cand = jnp.concatenate(cand, axis=0)
        eidx = jnp.concatenate(eidx, axis=0)
        cpos = jnp.concatenate(cpos, axis=0)
        best, sel = _top_rows(cand, cpos, kk)
        chosen = [jnp.max(jnp.where(cpos == p_, eidx, -1), axis=0, keepdims=True) for p_ in sel]
        best = jnp.concatenate(best, axis=0)
        e = jnp.exp(best - best[0:1])
        gate_rows.append(e / jnp.sum(e, axis=0, keepdims=True))
        idx_rows.append(jnp.concatenate(chosen, axis=0))
    idx = jnp.concatenate(idx_rows, axis=0)
    gates = jnp.concatenate(gate_rows, axis=0)
    idx_out[...] = idx.astype(F32).T.astype(jnp.int32)
    gate_out[...] = gates.T


def _peer_topk(q, keys):
    T = q.shape[0]
    c = min(TOPK_TILE, T)
    o_spec = pl.BlockSpec((c, N_SLOTS), lambda i: (i, 0))
    return pl.pallas_call(
        _peer_topk_kernel, grid=(T // c,),
        in_specs=[pl.BlockSpec((c, q.shape[1]), lambda i: (i, 0)), pl.BlockSpec(keys.shape, lambda i: (0, 0, 0))],
        out_specs=(o_spec, o_spec),
        out_shape=(jax.ShapeDtypeStruct((T, N_SLOTS), jnp.int32), jax.ShapeDtypeStruct((T, N_SLOTS), F32)),
        compiler_params=_cparams("parallel"), name="peer_topk")(q, keys)


HI_MASK = 0xFFFF0000


def _unpack(words):
    lo = lax.bitcast_convert_type(words << 16, F32)
    hi = lax.bitcast_convert_type(words & jnp.uint32(HI_MASK), F32)
    return lo, hi


def _pack_table(t):
    b = lax.bitcast_convert_type(t.astype(BF16), jnp.uint16).astype(jnp.uint32)
    words = b[:, :HALF_D] | (b[:, HALF_D:] << 16)
    return words.reshape(t.shape[0], PACK_ROWS, LANES)


def _expert_act_kernel(idx_ref, hn_ref, gate_ref, tab_ref, w_out, stage):
    def token(tt, carry):
        base = tt * N_SLOTS
        for j in range(N_SLOTS):
            stage[pl.ds(PACK_ROWS * j, PACK_ROWS), :] = tab_ref[idx_ref[0, 0, base + j]]
        x = hn_ref[tt]
        acc = jnp.zeros((N_SLOTS, LANES), F32)
        for s in range(PACK_ROWS):
            lo, hi = _unpack(stage[pl.ds(s, N_SLOTS, stride=PACK_ROWS), :])
            acc = acc + lo * x[s:s + 1] + hi * x[PACK_ROWS + s:PACK_ROWS + s + 1]
        act = jnp.sum(acc.T, axis=0, keepdims=True)
        gelu = 0.5 * act * (1.0 + lax.erf(act * (2.0 ** -0.5)))
        w_out[tt] = gate_ref[tt] * gelu
        return carry

    lax.fori_loop(0, hn_ref.shape[0], token, 0)


def _expert_act(idx3, hn3, gate3, table):
    T = hn3.shape[0]
    tb = idx3.shape[2] // N_SLOTS
    return pl.pallas_call(
        _expert_act_kernel, grid=(T // tb,),
        in_specs=[pl.BlockSpec((1, 1, tb * N_SLOTS), lambda i: (i, 0, 0), memory_space=pltpu.SMEM),
                  pl.BlockSpec((tb, SUBLANES, LANES), lambda i: (i, 0, 0)),
                  pl.BlockSpec((tb, 1, N_SLOTS), lambda i: (i, 0, 0)),
                  pl.BlockSpec(memory_space=pltpu.VMEM)],
        out_specs=pl.BlockSpec((tb, 1, N_SLOTS), lambda i: (i, 0, 0)),
        out_shape=jax.ShapeDtypeStruct((T, 1, N_SLOTS), F32),
        scratch_shapes=[pltpu.VMEM((N_SLOTS * PACK_ROWS, LANES), jnp.uint32)],
        compiler_params=_cparams("arbitrary"), name="expert_act")(idx3, hn3, gate3, table)


def _expert_sum_kernel(idx_ref, w_ref, tab_ref, o_ref):
    def token(tt, carry):
        base = tt * N_SLOTS
        acc_lo = jnp.zeros((PACK_ROWS, LANES), F32)
        acc_hi = jnp.zeros((PACK_ROWS, LANES), F32)
        for j in range(N_SLOTS):
            lo, hi = _unpack(tab_ref[idx_ref[0, 0, base + j]])
            w = w_ref[0, 0, base + j]
            acc_lo = acc_lo + w * lo
            acc_hi = acc_hi + w * hi
        o_ref[tt] = jnp.concatenate([acc_lo, acc_hi], axis=0)
        return carry

    lax.fori_loop(0, o_ref.shape[0], token, 0)


def _expert_sum(idx3, w3, table):
    tb = idx3.shape[2] // N_SLOTS
    T = idx3.shape[0] * tb
    smem = pl.BlockSpec((1, 1, tb * N_SLOTS), lambda i: (i, 0, 0), memory_space=pltpu.SMEM)
    return pl.pallas_call(
        _expert_sum_kernel, grid=(T // tb,),
        in_specs=[smem, smem, pl.BlockSpec(memory_space=pltpu.VMEM)],
        out_specs=pl.BlockSpec((tb, SUBLANES, LANES), lambda i: (i, 0, 0)),
        out_shape=jax.ShapeDtypeStruct((T, SUBLANES, LANES), F32),
        compiler_params=_cparams("arbitrary"), name="expert_sum")(idx3, w3, table)


def _final_kernel(h_ref, p_ref, g_ref, o_ref):
    o_ref[...] = _rms(h_ref[...] + p_ref[...], g_ref[...])


def _final_norm(h, p, g):
    T = h.shape[0]
    tm = min(ROW_TILE, T)
    row = pl.BlockSpec((tm, D_MODEL), lambda i: (i, 0))
    return pl.pallas_call(
        _final_kernel, grid=(T // tm,), in_specs=[row, row, pl.BlockSpec((1, D_MODEL), lambda i: (0, 0))],
        out_specs=row, out_shape=jax.ShapeDtypeStruct((T, D_MODEL), F32),
        compiler_params=_cparams("parallel"), name="final_norm")(h, p, g)


def _rope_tables(positions):
    half = ROPE_DIM // 2
    inv_freq = ROPE_THETA ** (-jnp.arange(0, ROPE_DIM, 2, dtype=F32) / ROPE_DIM)
    ang = positions.reshape(-1).astype(F32)[:, None] * inv_freq
    cos, sin = jnp.cos(ang), jnp.sin(ang)
    T = cos.shape[0]
    pad = HEAD_DIM - ROPE_DIM
    head = lambda first, second, fill: jnp.concatenate(
        [first, second, jnp.full((T, pad), fill, F32)], axis=1)
    zeros = jnp.zeros_like(sin)
    reps = LANES // HEAD_DIM
    return tuple(jnp.tile(head(*parts), (1, reps))
                 for parts in ((cos, cos, 1.0), (-sin, zeros, 0.0), (zeros, sin, 0.0)))


def _reorder_in_proj(w):
    a0 = ATTN_WIDTH
    a1, a2 = a0 + KV_WIDTH, a0 + 2 * KV_WIDTH
    a3 = a2 + 3 * HYENA_WIDTH
    q, k, v, hy, gates = w[:, :a0], w[:, a0:a1], w[:, a1:a2], w[:, a2:a3], w[:, a3:]
    return jnp.concatenate([gates, hy, q, k, v], axis=1).astype(BF16)


def kernel(x, positions, norm_mix, w_in, conv_w, conv_b, filt_w1, filt_b1, filt_freq1, filt_w2, filt_b2, filt_freq2,
           filt_w3, filt_b3, hyena_skip, attn_sink, w_attn_branch, w_hyena_branch, w_out, norm_ffn, w_query,
           sub_keys, expert_u, expert_v, norm_final):
    B, S, D = x.shape
    assert D == D_MODEL and S % BLOCK == 0
    T = B * S
    depth = w_in.shape[0]
    tb = min(GATHER_TOKENS, T)
    cos_t, s1_t, s2_t = _rope_tables(positions)
    fmat, gmat = _dft_matrices(S)
    z_emb, decay = _filter_inputs(S)

    h = x.reshape(T, D)
    peer = None
    for l in range(depth):
        res = _in_proj(h, peer, norm_mix[l][None], _reorder_in_proj(w_in[l]), cos_t, s1_t, s2_t)
        xcur, proj = (h, res) if peer is None else res
        attn = _attention(proj, attn_sink[l], B, S)
        x0, z, zb = _hyena_prep(proj, conv_w[l], conv_b[l][None], B, S)
        kre, kim = _hyena_filter_spectrum(S, z_emb, decay, fmat, filt_w1[l], filt_b1[l], filt_freq1[l], filt_w2[l],
                                          filt_b2[l], filt_freq2[l], filt_w3[l], filt_b3[l])
        ylong = _long_conv(zb, fmat, gmat, kre, kim, B, S)
        h, hn, q = _merge(xcur, attn, x0, z, ylong, proj, hyena_skip[l][None], w_attn_branch[l].astype(BF16),
                          w_hyena_branch[l].astype(BF16), w_out[l].astype(BF16), norm_ffn[l][None],
                          w_query[l].astype(BF16))
        idx, gates = _peer_topk(q, sub_keys[l].astype(BF16))
        idx3 = idx.reshape(T // tb, 1, tb * N_SLOTS)
        wts = _expert_act(idx3, hn.reshape(T, SUBLANES, LANES), gates.reshape(T, 1, N_SLOTS),
                          _pack_table(expert_u[l]))
        peer = _expert_sum(idx3, wts.reshape(T // tb, 1, tb * N_SLOTS), _pack_table(expert_v[l])).reshape(T, D)
    return _final_norm(h, peer, norm_final[None]).reshape(B, S, D)
```

```python
import functools
import math

import jax
import jax.numpy as jnp
from jax import lax
from jax.experimental import pallas as pl
from jax.experimental.pallas import tpu as pltpu

F32 = jnp.float32
BF16 = jnp.bfloat16

D_MODEL = 1024
N_Q_HEADS = 8
N_KV_HEADS = 2
HEAD_DIM = 64
GQA_GROUP = N_Q_HEADS // N_KV_HEADS
ATTN_WIDTH = N_Q_HEADS * HEAD_DIM
KV_WIDTH = N_KV_HEADS * HEAD_DIM
WINDOW = 128
BLOCK = 128
ROPE_DIM = HEAD_DIM // 4
ROPE_THETA = 500000.0
HYENA_WIDTH = 512
FILTER_EMB_DIM = 33
FILTER_HIDDEN = 64
DECAY_TARGET = 1e-2
MAX_DECAY = math.log(DECAY_TARGET) / 0.3
MIN_DECAY = math.log(DECAY_TARGET) / 1.5
N_KEYS = 128
N_EXPERTS = N_KEYS * N_KEYS
PEER_HEADS = 8
PEER_TOPK = 16
HALF_QUERY = 128
N_SLOTS = PEER_HEADS * PEER_TOPK
EPS = 1e-6

LANES = 128
SUBLANES = 8
VMEM_LIMIT_BYTES = 56 * 1024 * 1024

GATE_OFF = 0
HY_OFF = 2 * D_MODEL
Q_OFF = HY_OFF + 3 * HYENA_WIDTH
K_OFF = Q_OFF + ATTN_WIDTH
V_OFF = K_OFF + KV_WIDTH
IN_WIDTH = V_OFF + KV_WIDTH
ROPE_WIDTH = ATTN_WIDTH + KV_WIDTH

ROW_TILE = 256
FREQ_TILE = 512
TOPK_TILE = 256
GATHER_TOKENS = 64
HALF_D = D_MODEL // 2
PACK_ROWS = HALF_D // LANES


def _cparams(*sem):
    return pltpu.CompilerParams(dimension_semantics=sem, vmem_limit_bytes=VMEM_LIMIT_BYTES)


def _rms(x, g):
    r = lax.rsqrt(jnp.mean(x * x, axis=-1, keepdims=True) + EPS)
    return x * r * g


def _in_proj_kernel(*refs, has_add):
    if has_add:
        h_ref, p_ref, g_ref, w_ref, c_ref, s1_ref, s2_ref, x_out, proj_out = refs
        x = h_ref[...] + p_ref[...]
        x_out[...] = x
    else:
        h_ref, g_ref, w_ref, c_ref, s1_ref, s2_ref, proj_out = refs
        x = h_ref[...]
    xn = _rms(x, g_ref[...]).astype(BF16)
    proj_out[:, :Q_OFF] = jnp.dot(xn, w_ref[:, :Q_OFF], preferred_element_type=F32)
    qk = jnp.dot(xn, w_ref[:, Q_OFF:V_OFF], preferred_element_type=F32)
    reps = ROPE_WIDTH // LANES
    cos = jnp.concatenate([c_ref[...]] * reps, axis=1)
    s1 = jnp.concatenate([s1_ref[...]] * reps, axis=1)
    s2 = jnp.concatenate([s2_ref[...]] * reps, axis=1)
    half = ROPE_DIM // 2
    rot = qk * cos + pltpu.roll(qk, ROPE_WIDTH - half, axis=1) * s1 + pltpu.roll(qk, half, axis=1) * s2
    proj_out[:, Q_OFF:V_OFF] = rot
    proj_out[:, V_OFF:] = jnp.dot(xn, w_ref[:, V_OFF:], preferred_element_type=F32)


def _in_proj(h, p, g, w, cos_t, s1_t, s2_t):
    T = h.shape[0]
    tm = min(ROW_TILE, T)
    has_add = p is not None
    row = lambda i: (i, 0)
    fixed = lambda i: (0, 0)
    x_spec = pl.BlockSpec((tm, D_MODEL), row)
    tab_spec = pl.BlockSpec((tm, LANES), row)
    in_specs = [x_spec] + ([x_spec] if has_add else []) + [
        pl.BlockSpec((1, D_MODEL), fixed), pl.BlockSpec((D_MODEL, IN_WIDTH), fixed), tab_spec, tab_spec, tab_spec]
    proj_shape = jax.ShapeDtypeStruct((T, IN_WIDTH), F32)
    proj_spec = pl.BlockSpec((tm, IN_WIDTH), row)
    if has_add:
        out_shape, out_specs = (jax.ShapeDtypeStruct((T, D_MODEL), F32), proj_shape), (x_spec, proj_spec)
        args = (h, p, g, w, cos_t, s1_t, s2_t)
    else:
        out_shape, out_specs = proj_shape, proj_spec
        args = (h, g, w, cos_t, s1_t, s2_t)
    return pl.pallas_call(
        functools.partial(_in_proj_kernel, has_add=has_add), grid=(T // tm,), in_specs=in_specs,
        out_specs=out_specs, out_shape=out_shape, compiler_params=_cparams("parallel"), name="in_proj")(*args)


def _attn_kernel(sink_ref, q_ref, kp_ref, ko_ref, kn_ref, o_ref, *, nb):
    i = pl.program_id(1)
    q = q_ref[...]
    kv = jnp.concatenate([kp_ref[...], ko_ref[...], kn_ref[...]], axis=0)
    r = lax.broadcasted_iota(jnp.int32, (BLOCK, 3 * BLOCK), 0)
    c = lax.broadcasted_iota(jnp.int32, (BLOCK, 3 * BLOCK), 1)
    d = c - r
    valid = (d >= BLOCK - WINDOW) & (d <= BLOCK + WINDOW)
    valid &= (c >= BLOCK) | (i > 0)
    valid &= (c < 2 * BLOCK) | (i < nb - 1)
    valid = jnp.concatenate([valid] * GQA_GROUP, axis=0)
    scale = HEAD_DIM ** -0.5
    for g in range(N_KV_HEADS):
        k = kv[:, g * HEAD_DIM:(g + 1) * HEAD_DIM].astype(BF16)
        v = kv[:, KV_WIDTH + g * HEAD_DIM:KV_WIDTH + (g + 1) * HEAD_DIM].astype(BF16)
        heads = [g * GQA_GROUP + hh for hh in range(GQA_GROUP)]
        qs = jnp.concatenate([q[:, h * HEAD_DIM:(h + 1) * HEAD_DIM] for h in heads], axis=0).astype(BF16)
        s = lax.dot_general(qs, k, (((1,), (1,)), ((), ())), preferred_element_type=F32) * scale
        s = jnp.where(valid, s, -1e30)
        sink = jnp.concatenate([jnp.full((BLOCK, 1), sink_ref[h], F32) for h in heads], axis=0)
        m = jnp.maximum(jnp.max(s, axis=-1, keepdims=True), sink)
        p = jnp.exp(s - m)
        denom = jnp.sum(p, axis=-1, keepdims=True) + jnp.exp(sink - m)
        o = jnp.dot(p.astype(BF16), v, preferred_element_type=F32) / denom
        for hh, h in enumerate(heads):
            o_ref[:, h * HEAD_DIM:(h + 1) * HEAD_DIM] = o[hh * BLOCK:(hh + 1) * BLOCK].astype(o_ref.dtype)


def _attention(proj, sink, B, S):
    nb = S // BLOCK
    qcol = Q_OFF // ATTN_WIDTH
    kvcol = K_OFF // (2 * KV_WIDTH)
    kv_spec = lambda f: pl.BlockSpec((BLOCK, 2 * KV_WIDTH), f)
    return pl.pallas_call(
        functools.partial(_attn_kernel, nb=nb), grid=(B, nb),
        in_specs=[pl.BlockSpec(memory_space=pltpu.SMEM),
                  pl.BlockSpec((BLOCK, ATTN_WIDTH), lambda b, i: (b * nb + i, qcol)),
                  kv_spec(lambda b, i: (b * nb + jnp.maximum(i - 1, 0), kvcol)),
                  kv_spec(lambda b, i: (b * nb + i, kvcol)),
                  kv_spec(lambda b, i: (b * nb + jnp.minimum(i + 1, nb - 1), kvcol))],
        out_specs=pl.BlockSpec((BLOCK, ATTN_WIDTH), lambda b, i: (b * nb + i, 0)),
        out_shape=jax.ShapeDtypeStruct((B * S, ATTN_WIDTH), BF16),
        compiler_params=_cparams("parallel", "parallel"), name="window_attn")(sink, proj, proj, proj, proj)


HY_COLS = 256


def _short_conv(u, w, b):
    L = u.shape[0]
    row = lax.broadcasted_iota(jnp.int32, u.shape, 0)
    prev = jnp.where(row == 0, 0.0, pltpu.roll(u, 1, axis=0))
    nxt = jnp.where(row == L - 1, 0.0, pltpu.roll(u, L - 1, axis=0))
    return w[0:1] * prev + w[1:2] * u + w[2:3] * nxt + b


def _hyena_prep_kernel(u0, u1, u2, w0, w1, w2, b0, b1, b2, x0_out, z_out, zb_out):
    x0_out[...] = _short_conv(u0[...], w0[...], b0[...])
    z = _short_conv(u1[...], w1[...], b1[...]) * _short_conv(u2[...], w2[...], b2[...])
    z_out[...] = z
    zb_out[...] = z.astype(BF16)


def _hyena_prep(proj, conv_w, conv_b, B, L):
    nc = HYENA_WIDTH // HY_COLS
    base = HY_OFF // HY_COLS
    u_spec = lambda part: pl.BlockSpec((L, HY_COLS), lambda b, c: (b, base + part * nc + c))
    w_spec = lambda part: pl.BlockSpec((3, HY_COLS), lambda b, c: (0, part * nc + c))
    b_spec = lambda part: pl.BlockSpec((1, HY_COLS), lambda b, c: (0, part * nc + c))
    o_spec = pl.BlockSpec((L, HY_COLS), lambda b, c: (b, c))
    shp = lambda dt: jax.ShapeDtypeStruct((B * L, HYENA_WIDTH), dt)
    return pl.pallas_call(
        _hyena_prep_kernel, grid=(B, nc),
        in_specs=[u_spec(0), u_spec(1), u_spec(2), w_spec(0), w_spec(1), w_spec(2), b_spec(0), b_spec(1), b_spec(2)],
        out_specs=(o_spec, o_spec, o_spec), out_shape=(shp(F32), shp(F32), shp(BF16)),
        compiler_params=_cparams("parallel", "parallel"), name="hyena_prep",
    )(proj, proj, proj, conv_w, conv_w, conv_w, conv_b, conv_b, conv_b)


def _filter_mlp_kernel(z_ref, w1, b1, f1, w2, b2, f2, w3, b3, dec_ref, o_ref):
    h = jnp.sin(f1[...] * (jnp.dot(z_ref[...], w1[...], preferred_element_type=F32) + b1[...]))
    h = jnp.sin(f2[...] * (jnp.dot(h, w2[...], preferred_element_type=F32) + b2[...]))
    h = jnp.dot(h, w3[...], preferred_element_type=F32) + b3[...]
    dec = dec_ref[...]
    o_ref[...] = h * jnp.concatenate([dec, dec], axis=1)


def _filter_mlp(z, w1, b1, f1, w2, b2, f2, w3, b3, decay):
    L = z.shape[0]
    tl = min(512, L)
    fixed = lambda a: pl.BlockSpec(a.shape, lambda i: (0, 0))
    return pl.pallas_call(
        _filter_mlp_kernel, grid=(L // tl,),
        in_specs=[pl.BlockSpec((tl, LANES), lambda i: (i, 0))] + [fixed(a) for a in (w1, b1, f1, w2, b2, f2, w3, b3)]
        + [pl.BlockSpec((tl, HYENA_WIDTH), lambda i: (i, 0))],
        out_specs=pl.BlockSpec((tl, 2 * HYENA_WIDTH), lambda i: (i, 0)),
        out_shape=jax.ShapeDtypeStruct((L, 2 * HYENA_WIDTH), F32),
        compiler_params=_cparams("parallel"), name="filter_mlp")(z, w1, b1, f1, w2, b2, f2, w3, b3, decay)


def _split_bf16(a):
    hi = a.astype(BF16)
    return hi, (a - hi.astype(F32)).astype(BF16)


def _filter_dft_kernel(fre, fim, k0_ref, k1_ref, o_re, o_im, *, n_fft):
    ft = fre.shape[0]
    k = pl.program_id(0) * ft + lax.broadcasted_iota(jnp.int32, (ft, 1), 0)
    sign = (1 - 2 * (k & 1)).astype(F32)
    amp = jnp.where(k == 0, 1.0 / n_fft, 2.0 / n_fft)
    k0h, k0l = _split_bf16(k0_ref[...])
    k1h, k1l = _split_bf16(k1_ref[...])

    def part(f):
        d = lambda a: jnp.dot(f, a, preferred_element_type=F32)
        return amp * ((d(k0h) + d(k0l)) + sign * (d(k1h) + d(k1l)))

    o_re[...] = part(fre[...])
    o_im[...] = part(fim[...])


def _filter_dft(fmat, k0, k1):
    L = k0.shape[0]
    ft = min(FREQ_TILE, L)
    nf = L // ft
    half = pl.BlockSpec((L, HYENA_WIDTH), lambda f: (0, 0))
    o_spec = pl.BlockSpec((ft, HYENA_WIDTH), lambda f: (f, 0))
    shp = jax.ShapeDtypeStruct((L, HYENA_WIDTH), F32)
    return pl.pallas_call(
        functools.partial(_filter_dft_kernel, n_fft=2 * L), grid=(nf,),
        in_specs=[pl.BlockSpec((ft, L), lambda f: (f, 0)), pl.BlockSpec((ft, L), lambda f: (nf + f, 0)), half, half],
        out_specs=(o_spec, o_spec), out_shape=(shp, shp),
        compiler_params=_cparams("parallel"), name="filter_dft")(fmat, fmat, k0, k1)


def _long_conv_kernel(z_ref, fre, fim, gre, gim, kre_ref, kim_ref, y_ref):
    f = pl.program_id(1)
    z = z_ref[...]
    zre = jnp.dot(fre[...], z, preferred_element_type=F32)
    zim = jnp.dot(fim[...], z, preferred_element_type=F32)
    kre, kim = kre_ref[...], kim_ref[...]
    a, b, c, d = zre * kre, zim * kim, zre * kim, zim * kre
    real_row = (lax.broadcasted_iota(jnp.int32, zre.shape, 0) == 0) & (f == 0)
    pre = jnp.where(real_row, a, a - b).astype(BF16)
    pim = jnp.where(real_row, b, c + d).astype(BF16)
    y = jnp.dot(gre[...], pre, preferred_element_type=F32) + jnp.dot(gim[...], pim, preferred_element_type=F32)

    @pl.when(f == 0)
    def _():
        y_ref[...] = y

    @pl.when(f > 0)
    def _():
        y_ref[...] += y


def _long_conv(zb, fmat, gmat, kre, kim, B, L):
    ft = min(FREQ_TILE, L)
    nf = L // ft
    k_spec = pl.BlockSpec((ft, HYENA_WIDTH), lambda b, f: (f, 0))
    return pl.pallas_call(
        _long_conv_kernel, grid=(B, nf),
        in_specs=[pl.BlockSpec((L, HYENA_WIDTH), lambda b, f: (b, 0)),
                  pl.BlockSpec((ft, L), lambda b, f: (f, 0)), pl.BlockSpec((ft, L), lambda b, f: (nf + f, 0)),
                  pl.BlockSpec((L, ft), lambda b, f: (0, f)), pl.BlockSpec((L, ft), lambda b, f: (0, nf + f)),
                  k_spec, k_spec],
        out_specs=pl.BlockSpec((L, HYENA_WIDTH), lambda b, f: (b, 0)),
        out_shape=jax.ShapeDtypeStruct((B * L, HYENA_WIDTH), F32),
        compiler_params=_cparams("parallel", "arbitrary"), name="long_conv")(zb, fmat, fmat, gmat, gmat, kre, kim)


def _dft_matrices(L):
    n = 2 * L

    def build(k, t):
        ang = ((k * t) % n).astype(F32) * (2.0 * math.pi / n)
        re = jnp.cos(ang)
        im = jnp.where(k == 0, (1 - 2 * (t & 1)).astype(F32), -jnp.sin(ang))
        return re.astype(BF16), im.astype(BF16)

    col = jnp.arange(L, dtype=jnp.int32)[:, None]
    row = jnp.arange(L, dtype=jnp.int32)[None, :]
    fre, fim = build(col, row)
    gre, gim = build(row, col)
    return jnp.concatenate([fre, fim], axis=0), jnp.concatenate([gre, gim], axis=1)


def _filter_inputs(L):
    t = jnp.linspace(0.0, 1.0, L, dtype=F32)[:, None]
    bands = (FILTER_EMB_DIM - 1) // 2
    w = 2.0 * math.pi * jnp.arange(L, dtype=F32)[:, None] / L
    f = jnp.linspace(1e-4, bands - 1, bands, dtype=F32)[None, :]
    z = jnp.concatenate([t, jnp.cos(f * w), -jnp.sin(f * w)], axis=-1)
    deltas = jnp.linspace(MIN_DECAY, MAX_DECAY, HYENA_WIDTH, dtype=F32)
    decay = jnp.exp(-t * jnp.abs(deltas)[None, :])
    return jnp.pad(z, ((0, 0), (0, LANES - FILTER_EMB_DIM))), decay


def _pad_to(a, rows, cols):
    return jnp.pad(a, ((0, rows - a.shape[0]), (0, cols - a.shape[1])))


def _hyena_filter_spectrum(L, z_emb, decay, fmat, w1, b1, fr1, w2, b2, fr2, w3, b3):
    h = _filter_mlp(z_emb, _pad_to(w1, LANES, LANES), _pad_to(b1[None], 1, LANES), _pad_to(fr1[None], 1, LANES),
                    _pad_to(w2, LANES, LANES), _pad_to(b2[None], 1, LANES), _pad_to(fr2[None], 1, LANES),
                    _pad_to(w3, LANES, 2 * HYENA_WIDTH), b3[None], decay)
    h_f, h_b = h[:, :HYENA_WIDTH], h[:, HYENA_WIDTH:]
    k0 = jnp.concatenate([h_f[:1] + h_b[:1], h_f[1:]], axis=0)
    k1 = jnp.concatenate([jnp.zeros((1, HYENA_WIDTH), F32), h_b[:0:-1]], axis=0)
    return _filter_dft(fmat, k0, k1)


def _merge_kernel(x_ref, a_ref, x0_ref, z_ref, y_ref, g_ref, skip_ref, wab, whb, wout, nf_ref, wq, h_out, hn_out,
                  q_out):
    ya = jnp.dot(a_ref[...], wab[...], preferred_element_type=F32)
    hy = x0_ref[...] * (y_ref[...] + z_ref[...] * skip_ref[...])
    yh = jnp.dot(hy.astype(BF16), whb[...], preferred_element_type=F32)
    g = g_ref[...]
    merged = jax.nn.sigmoid(g[:, :D_MODEL]) * ya + jax.nn.sigmoid(g[:, D_MODEL:]) * yh
    h = x_ref[...] + jnp.dot(merged.astype(BF16), wout[...], preferred_element_type=F32)
    h_out[...] = h
    hn = _rms(h, nf_ref[...])
    hn_out[...] = hn
    q_out[...] = jnp.dot(hn.astype(BF16), wq[...], preferred_element_type=F32)


def _merge(x, attn, x0, z, y, proj, skip, wab, whb, wout, nf, wq):
    T = x.shape[0]
    tm = min(ROW_TILE, T)
    row = lambda w: pl.BlockSpec((tm, w), lambda i: (i, 0))
    fixed = lambda a: pl.BlockSpec(a.shape, lambda i: (0, 0))
    qw = wq.shape[1]
    return pl.pallas_call(
        _merge_kernel, grid=(T // tm,),
        in_specs=[row(D_MODEL), row(ATTN_WIDTH), row(HYENA_WIDTH), row(HYENA_WIDTH), row(HYENA_WIDTH),
                  row(2 * D_MODEL), fixed(skip), fixed(wab), fixed(whb), fixed(wout), fixed(nf), fixed(wq)],
        out_specs=(row(D_MODEL), row(D_MODEL), row(qw)),
        out_shape=(jax.ShapeDtypeStruct((T, D_MODEL), F32), jax.ShapeDtypeStruct((T, D_MODEL), F32),
                   jax.ShapeDtypeStruct((T, qw), F32)),
        compiler_params=_cparams("parallel"), name="merge_query")(x, attn, x0, z, y, proj, skip, wab, whb, wout, nf, wq)


NEG_INF = float("-inf")


def _top_rows(s, pos, k):
    big = jnp.int32(2 ** 30)
    vals, sel = [], []
    for _ in range(k):
        m = jnp.max(s, axis=0, keepdims=True)
        p = jnp.min(jnp.where(s == m, pos, big), axis=0, keepdims=True)
        vals.append(m)
        sel.append(p)
        s = jnp.where(pos == p, NEG_INF, s)
    return vals, sel


def _peer_topk_kernel(q_ref, keys_ref, idx_out, gate_out):
    c = q_ref.shape[0]
    kk = PEER_TOPK
    key_pos = lax.broadcasted_iota(jnp.int32, (N_KEYS, c), 0)
    a_small = kk // 2
    b_iota = lax.broadcasted_iota(jnp.int32, (a_small, c), 0)
    idx_rows, gate_rows = [], []
    for h in range(PEER_HEADS):
        tops = []
        for p in range(2):
            qhp = q_ref[:, (2 * h + p) * HALF_QUERY:(2 * h + p + 1) * HALF_QUERY].astype(BF16)
            s = lax.dot_general(keys_ref[p], qhp, (((1,), (1,)), ((), ())), preferred_element_type=F32)
            vals, sel = _top_rows(s, key_pos, kk)
            tops.append((jnp.concatenate(vals, axis=0), jnp.concatenate(sel, axis=0)))
        (s0, i0), (s1, i1) = tops
        cand = [s0[0:1] + s1]
        eidx = [i0[0:1] * N_KEYS + i1]
        cpos = [lax.broadcasted_iota(jnp.int32, (kk, c), 0)]
        for a in range(1, a_small):
            keep = (a + 1) * (b_iota + 1) <= kk
            cand.append(jnp.where(keep, s0[a:a + 1] + s1[:a_small], NEG_INF))
            eidx.append(i0[a:a + 1] * N_KEYS + i1[:a_small])
            cpos.append(a * kk + b_iota)
        cand.append(s0[a_small:] + s1[0:1])
        eidx.append(i0[a_small:] * N_KEYS + i1[0:1])
        cpos.append((b_iota + a_small) * kk)
        cand = jnp.concatenate(cand, axis=0)
        eidx = jnp.concatenate(eidx, axis=0)
        cpos = jnp.concatenate(cpos, axis=0)
        best, sel = _top_rows(cand, cpos, kk)
        chosen = [jnp.max(jnp.where(cpos == p_, eidx, -1), axis=0, keepdims=True) for p_ in sel]
        best = jnp.concatenate(best, axis=0)
        e = jnp.exp(best - best[0:1])
        gate_rows.append(e / jnp.sum(e, axis=0, keepdims=True))
        idx_rows.append(jnp.concatenate(chosen, axis=0))
    idx = jnp.concatenate(idx_rows, axis=0)
    gates = jnp.concatenate(gate_rows, axis=0)
    idx_out[...] = idx.astype(F32).T.astype(jnp.int32)
    gate_out[...] = gates.T


def _peer_topk(q, keys):
    T = q.shape[0]
    c = min(TOPK_TILE, T)
    o_spec = pl.BlockSpec((c, N_SLOTS), lambda i: (i, 0))
    return pl.pallas_call(
        _peer_topk_kernel, grid=(T // c,),
        in_specs=[pl.BlockSpec((c, q.shape[1]), lambda i: (i, 0)), pl.BlockSpec(keys.shape, lambda i: (0, 0, 0))],
        out_specs=(o_spec, o_spec),
        out_shape=(jax.ShapeDtypeStruct((T, N_SLOTS), jnp.int32), jax.ShapeDtypeStruct((T, N_SLOTS), F32)),
        compiler_params=_cparams("parallel"), name="peer_topk")(q, keys)


HI_MASK = 0xFFFF0000


def _unpack(words):
    lo = lax.bitcast_convert_type(words << 16, F32)
    hi = lax.bitcast_convert_type(words & jnp.uint32(HI_MASK), F32)
    return lo, hi


def _pack_table(t):
    b = lax.bitcast_convert_type(t.astype(BF16), jnp.uint16).astype(jnp.uint32)
    words = b[:, :HALF_D] | (b[:, HALF_D:] << 16)
    return words.reshape(t.shape[0], PACK_ROWS, LANES)


def _gather_rows(idx_ref, tab_ref, tt, stage):
    tok_idx = idx_ref.at[0, 0, pl.ds(tt * N_SLOTS, N_SLOTS)]
    for j in range(N_SLOTS):
        stage[pl.ds(PACK_ROWS * j, PACK_ROWS), :] = tab_ref[tok_idx[j]]


def _staged_chunks(stage):
    for s in range(PACK_ROWS):
        lo, hi = _unpack(stage[pl.ds(s, N_SLOTS, stride=PACK_ROWS), :])
        yield s, lo, hi


def _two_token_pipeline(n_tokens, gather, compute, stages):
    last = n_tokens - 1
    gather(0, stages[0])

    def pair(i, carry):
        t0 = 2 * i
        gather(t0 + 1, stages[1])
        compute(t0, stages[0])
        gather(jnp.minimum(t0 + 2, last), stages[0])
        compute(t0 + 1, stages[1])
        return carry

    lax.fori_loop(0, n_tokens // 2, pair, 0)


def _expert_act_kernel(idx_ref, hn_ref, gate_ref, tab_ref, w_out, stage_a, stage_b):
    def compute(tt, stage):
        x = hn_ref[tt]
        acc = jnp.zeros((N_SLOTS, LANES), F32)
        for s, lo, hi in _staged_chunks(stage):
            acc = acc + lo * x[s:s + 1] + hi * x[PACK_ROWS + s:PACK_ROWS + s + 1]
        act = jnp.sum(acc.T, axis=0, keepdims=True)
        gelu = 0.5 * act * (1.0 + lax.erf(act * (2.0 ** -0.5)))
        w_out[tt] = gate_ref[tt] * gelu

    _two_token_pipeline(hn_ref.shape[0], functools.partial(_gather_rows, idx_ref, tab_ref), compute,
                        (stage_a, stage_b))


def _stage_scratch():
    return pltpu.VMEM((N_SLOTS * PACK_ROWS, LANES), jnp.uint32)


def _expert_act(idx3, hn3, gate3, table):
    T = hn3.shape[0]
    tb = idx3.shape[2] // N_SLOTS
    return pl.pallas_call(
        _expert_act_kernel, grid=(T // tb,),
        in_specs=[pl.BlockSpec((1, 1, tb * N_SLOTS), lambda i: (i, 0, 0), memory_space=pltpu.SMEM),
                  pl.BlockSpec((tb, SUBLANES, LANES), lambda i: (i, 0, 0)),
                  pl.BlockSpec((tb, 1, N_SLOTS), lambda i: (i, 0, 0)),
                  pl.BlockSpec(memory_space=pltpu.VMEM)],
        out_specs=pl.BlockSpec((tb, 1, N_SLOTS), lambda i: (i, 0, 0)),
        out_shape=jax.ShapeDtypeStruct((T, 1, N_SLOTS), F32),
        scratch_shapes=[_stage_scratch(), _stage_scratch()],
        compiler_params=_cparams("arbitrary"), name="expert_act")(idx3, hn3, gate3, table)


def _expert_sum_kernel(idx_ref, w_ref, tab_ref, o_ref, stage_a, stage_b):
    def compute(tt, stage):
        w_col = jnp.broadcast_to(w_ref[tt], (LANES, N_SLOTS)).T
        lo_rows, hi_rows = [], []
        for _, lo, hi in _staged_chunks(stage):
            lo_rows.append(jnp.sum(w_col * lo, axis=0, keepdims=True))
            hi_rows.append(jnp.sum(w_col * hi, axis=0, keepdims=True))
        o_ref[tt] = jnp.concatenate(lo_rows + hi_rows, axis=0)

    _two_token_pipeline(o_ref.shape[0], functools.partial(_gather_rows, idx_ref, tab_ref), compute,
                        (stage_a, stage_b))


def _expert_sum(idx3, w3, table):
    T = w3.shape[0]
    tb = idx3.shape[2] // N_SLOTS
    return pl.pallas_call(
        _expert_sum_kernel, grid=(T // tb,),
        in_specs=[pl.BlockSpec((1, 1, tb * N_SLOTS), lambda i: (i, 0, 0), memory_space=pltpu.SMEM),
                  pl.BlockSpec((tb, 1, N_SLOTS), lambda i: (i, 0, 0)),
                  pl.BlockSpec(memory_space=pltpu.VMEM)],
        out_specs=pl.BlockSpec((tb, SUBLANES, LANES), lambda i: (i, 0, 0)),
        out_shape=jax.ShapeDtypeStruct((T, SUBLANES, LANES), F32),
        scratch_shapes=[_stage_scratch(), _stage_scratch()],
        compiler_params=_cparams("arbitrary"), name="expert_sum")(idx3, w3, table)


def _final_kernel(h_ref, p_ref, g_ref, o_ref):
    o_ref[...] = _rms(h_ref[...] + p_ref[...], g_ref[...])


def _final_norm(h, p, g):
    T = h.shape[0]
    tm = min(ROW_TILE, T)
    row = pl.BlockSpec((tm, D_MODEL), lambda i: (i, 0))
    return pl.pallas_call(
        _final_kernel, grid=(T // tm,), in_specs=[row, row, pl.BlockSpec((1, D_MODEL), lambda i: (0, 0))],
        out_specs=row, out_shape=jax.ShapeDtypeStruct((T, D_MODEL), F32),
        compiler_params=_cparams("parallel"), name="final_norm")(h, p, g)


def _rope_tables(positions):
    half = ROPE_DIM // 2
    inv_freq = ROPE_THETA ** (-jnp.arange(0, ROPE_DIM, 2, dtype=F32) / ROPE_DIM)
    ang = positions.reshape(-1).astype(F32)[:, None] * inv_freq
    cos, sin = jnp.cos(ang), jnp.sin(ang)
    T = cos.shape[0]
    pad = HEAD_DIM - ROPE_DIM
    head = lambda first, second, fill: jnp.concatenate(
        [first, second, jnp.full((T, pad), fill, F32)], axis=1)
    zeros = jnp.zeros_like(sin)
    reps = LANES // HEAD_DIM
    return tuple(jnp.tile(head(*parts), (1, reps))
                 for parts in ((cos, cos, 1.0), (-sin, zeros, 0.0), (zeros, sin, 0.0)))


def _reorder_in_proj(w):
    a0 = ATTN_WIDTH
    a1, a2 = a0 + KV_WIDTH, a0 + 2 * KV_WIDTH
    a3 = a2 + 3 * HYENA_WIDTH
    q, k, v, hy, gates = w[:, :a0], w[:, a0:a1], w[:, a1:a2], w[:, a2:a3], w[:, a3:]
    return jnp.concatenate([gates, hy, q, k, v], axis=1).astype(BF16)


def kernel(x, positions, norm_mix, w_in, conv_w, conv_b, filt_w1, filt_b1, filt_freq1, filt_w2, filt_b2, filt_freq2,
           filt_w3, filt_b3, hyena_skip, attn_sink, w_attn_branch, w_hyena_branch, w_out, norm_ffn, w_query,
           sub_keys, expert_u, expert_v, norm_final):
    B, S, D = x.shape
    assert D == D_MODEL and S % BLOCK == 0
    T = B * S
    depth = w_in.shape[0]
    tb = min(GATHER_TOKENS, T)
    cos_t, s1_t, s2_t = _rope_tables(positions)
    fmat, gmat = _dft_matrices(S)
    z_emb, decay = _filter_inputs(S)

    h = x.reshape(T, D)
    peer = None
    for l in range(depth):
        res = _in_proj(h, peer, norm_mix[l][None], _reorder_in_proj(w_in[l]), cos_t, s1_t, s2_t)
        xcur, proj = (h, res) if peer is None else res
        attn = _attention(proj, attn_sink[l], B, S)
        x0, z, zb = _hyena_prep(proj, conv_w[l], conv_b[l][None], B, S)
        kre, kim = _hyena_filter_spectrum(S, z_emb, decay, fmat, filt_w1[l], filt_b1[l], filt_freq1[l], filt_w2[l],
                                          filt_b2[l], filt_freq2[l], filt_w3[l], filt_b3[l])
        ylong = _long_conv(zb, fmat, gmat, kre, kim, B, S)
        h, hn, q = _merge(xcur, attn, x0, z, ylong, proj, hyena_skip[l][None], w_attn_branch[l].astype(BF16),
                          w_hyena_branch[l].astype(BF16), w_out[l].astype(BF16), norm_ffn[l][None],
                          w_query[l].astype(BF16))
        idx, gates = _peer_topk(q, sub_keys[l].astype(BF16))
        idx3 = idx.reshape(T // tb, 1, tb * N_SLOTS)
        wts = _expert_act(idx3, hn.reshape(T, SUBLANES, LANES), gates.reshape(T, 1, N_SLOTS),
                          _pack_table(expert_u[l]))
        peer = _expert_sum(idx3, wts, _pack_table(expert_v[l])).reshape(T, D)
    return _final_norm(h, peer, norm_final[None]).reshape(B, S, D)
```

```python
import functools
import math

import jax
import jax.numpy as jnp
from jax import lax
from jax.experimental import pallas as pl
from jax.experimental.pallas import tpu as pltpu

F32 = jnp.float32
BF16 = jnp.bfloat16

D_MODEL = 1024
N_Q_HEADS = 8
N_KV_HEADS = 2
HEAD_DIM = 64
GQA_GROUP = N_Q_HEADS // N_KV_HEADS
ATTN_WIDTH = N_Q_HEADS * HEAD_DIM
KV_WIDTH = N_KV_HEADS * HEAD_DIM
WINDOW = 128
BLOCK = 128
ROPE_DIM = HEAD_DIM // 4
ROPE_THETA = 500000.0
HYENA_WIDTH = 512
FILTER_EMB_DIM = 33
FILTER_HIDDEN = 64
DECAY_TARGET = 1e-2
MAX_DECAY = math.log(DECAY_TARGET) / 0.3
MIN_DECAY = math.log(DECAY_TARGET) / 1.5
N_KEYS = 128
N_EXPERTS = N_KEYS * N_KEYS
PEER_HEADS = 8
PEER_TOPK = 16
HALF_QUERY = 128
N_SLOTS = PEER_HEADS * PEER_TOPK
EPS = 1e-6

LANES = 128
SUBLANES = 8
VMEM_LIMIT_BYTES = 56 * 1024 * 1024

GATE_OFF = 0
HY_OFF = 2 * D_MODEL
Q_OFF = HY_OFF + 3 * HYENA_WIDTH
K_OFF = Q_OFF + ATTN_WIDTH
V_OFF = K_OFF + KV_WIDTH
IN_WIDTH = V_OFF + KV_WIDTH
ROPE_WIDTH = ATTN_WIDTH + KV_WIDTH

ROW_TILE = 256
FREQ_TILE = 512
TOPK_TILE = 256
GATHER_TOKENS = 64
HALF_D = D_MODEL // 2
PACK_ROWS = HALF_D // LANES


def _cparams(*sem):
    return pltpu.CompilerParams(dimension_semantics=sem, vmem_limit_bytes=VMEM_LIMIT_BYTES)


def _rms(x, g):
    r = lax.rsqrt(jnp.mean(x * x, axis=-1, keepdims=True) + EPS)
    return x * r * g


def _in_proj_kernel(*refs, has_add):
    if has_add:
        h_ref, p_ref, g_ref, w_ref, c_ref, s1_ref, s2_ref, x_out, proj_out = refs
        x = h_ref[...] + p_ref[...]
        x_out[...] = x
    else:
        h_ref, g_ref, w_ref, c_ref, s1_ref, s2_ref, proj_out = refs
        x = h_ref[...]
    xn = _rms(x, g_ref[...]).astype(BF16)
    proj_out[:, :Q_OFF] = jnp.dot(xn, w_ref[:, :Q_OFF], preferred_element_type=F32)
    qk = jnp.dot(xn, w_ref[:, Q_OFF:V_OFF], preferred_element_type=F32)
    reps = ROPE_WIDTH // LANES
    cos = jnp.concatenate([c_ref[...]] * reps, axis=1)
    s1 = jnp.concatenate([s1_ref[...]] * reps, axis=1)
    s2 = jnp.concatenate([s2_ref[...]] * reps, axis=1)
    half = ROPE_DIM // 2
    rot = qk * cos + pltpu.roll(qk, ROPE_WIDTH - half, axis=1) * s1 + pltpu.roll(qk, half, axis=1) * s2
    proj_out[:, Q_OFF:V_OFF] = rot
    proj_out[:, V_OFF:] = jnp.dot(xn, w_ref[:, V_OFF:], preferred_element_type=F32)


def _in_proj(h, p, g, w, cos_t, s1_t, s2_t):
    T = h.shape[0]
    tm = min(ROW_TILE, T)
    has_add = p is not None
    row = lambda i: (i, 0)
    fixed = lambda i: (0, 0)
    x_spec = pl.BlockSpec((tm, D_MODEL), row)
    tab_spec = pl.BlockSpec((tm, LANES), row)
    in_specs = [x_spec] + ([x_spec] if has_add else []) + [
        pl.BlockSpec((1, D_MODEL), fixed), pl.BlockSpec((D_MODEL, IN_WIDTH), fixed), tab_spec, tab_spec, tab_spec]
    proj_shape = jax.ShapeDtypeStruct((T, IN_WIDTH), F32)
    proj_spec = pl.BlockSpec((tm, IN_WIDTH), row)
    if has_add:
        out_shape, out_specs = (jax.ShapeDtypeStruct((T, D_MODEL), F32), proj_shape), (x_spec, proj_spec)
        args = (h, p, g, w, cos_t, s1_t, s2_t)
    else:
        out_shape, out_specs = proj_shape, proj_spec
        args = (h, g, w, cos_t, s1_t, s2_t)
    return pl.pallas_call(
        functools.partial(_in_proj_kernel, has_add=has_add), grid=(T // tm,), in_specs=in_specs,
        out_specs=out_specs, out_shape=out_shape, compiler_params=_cparams("parallel"), name="in_proj")(*args)


ATTN_Q_BLOCKS = 2


def _attn_kernel(sink_ref, q_ref, kp_ref, ko_ref, kn_ref, o_ref, *, n_steps):
    i = pl.program_id(1)
    kv = jnp.concatenate([kp_ref[...], ko_ref[...], kn_ref[...]], axis=0)
    c = lax.broadcasted_iota(jnp.int32, (3 * BLOCK, BLOCK), 0)
    r = lax.broadcasted_iota(jnp.int32, (3 * BLOCK, BLOCK), 1)
    d = c - r
    band = (d >= BLOCK - WINDOW) & (d <= BLOCK + WINDOW)
    scale = HEAD_DIM ** -0.5
    for u in range(ATTN_Q_BLOCKS):
        valid = band
        if u == 0:
            valid = valid & ((c >= BLOCK) | (i > 0))
        if u == ATTN_Q_BLOCKS - 1:
            valid = valid & ((c < 2 * BLOCK) | (i < n_steps - 1))
        valid = jnp.concatenate([valid] * GQA_GROUP, axis=1)
        q = q_ref[u * BLOCK:(u + 1) * BLOCK, :] * scale
        kv_u = kv[u * BLOCK:(u + 3) * BLOCK]
        v_t = kv_u[:, KV_WIDTH:].T.astype(BF16)
        for g in range(N_KV_HEADS):
            k = kv_u[:, g * HEAD_DIM:(g + 1) * HEAD_DIM].astype(BF16)
            heads = [g * GQA_GROUP + hh for hh in range(GQA_GROUP)]
            qs = jnp.concatenate([q[:, h * HEAD_DIM:(h + 1) * HEAD_DIM] for h in heads], axis=0).astype(BF16)
            s = lax.dot_general(k, qs, (((1,), (1,)), ((), ())), preferred_element_type=F32)
            s = jnp.where(valid, s, -1e30)
            sink = jnp.concatenate([jnp.full((1, BLOCK), sink_ref[h], F32) for h in heads], axis=1)
            m = jnp.maximum(jnp.max(s, axis=0, keepdims=True), sink)
            p = jnp.exp(s - m)
            denom = jnp.sum(p, axis=0, keepdims=True) + jnp.exp(sink - m)
            o_t = jnp.dot(v_t[g * HEAD_DIM:(g + 1) * HEAD_DIM], p.astype(BF16), preferred_element_type=F32)
            o_t = o_t * (1.0 / denom)
            for hh in range(0, GQA_GROUP, 2):
                pair = jnp.concatenate([o_t[:, hh * BLOCK:(hh + 1) * BLOCK],
                                        o_t[:, (hh + 1) * BLOCK:(hh + 2) * BLOCK]], axis=0).T
                col = heads[hh] * HEAD_DIM
                o_ref[u * BLOCK:(u + 1) * BLOCK, col:col + 2 * HEAD_DIM] = pair.astype(o_ref.dtype)


def _attention(proj, sink, B, S):
    nb = S // BLOCK
    nq = ATTN_Q_BLOCKS
    assert nb % nq == 0
    ns = nb // nq
    qcol = Q_OFF // ATTN_WIDTH
    kvcol = K_OFF // (2 * KV_WIDTH)
    edge_spec = lambda f: pl.BlockSpec((BLOCK, 2 * KV_WIDTH), f)
    return pl.pallas_call(
        functools.partial(_attn_kernel, n_steps=ns), grid=(B, ns),
        in_specs=[pl.BlockSpec(memory_space=pltpu.SMEM),
                  pl.BlockSpec((nq * BLOCK, ATTN_WIDTH), lambda b, i: (b * ns + i, qcol)),
                  edge_spec(lambda b, i: (b * nb + jnp.maximum(i * nq - 1, 0), kvcol)),
                  pl.BlockSpec((nq * BLOCK, 2 * KV_WIDTH), lambda b, i: (b * ns + i, kvcol)),
                  edge_spec(lambda b, i: (b * nb + jnp.minimum((i + 1) * nq, nb - 1), kvcol))],
        out_specs=pl.BlockSpec((nq * BLOCK, ATTN_WIDTH), lambda b, i: (b * ns + i, 0)),
        out_shape=jax.ShapeDtypeStruct((B * S, ATTN_WIDTH), BF16),
        compiler_params=_cparams("parallel", "parallel"), name="window_attn")(sink, proj, proj, proj, proj)


HY_COLS = 256


def _short_conv(u, w, b):
    L = u.shape[0]
    row = lax.broadcasted_iota(jnp.int32, u.shape, 0)
    prev = jnp.where(row == 0, 0.0, pltpu.roll(u, 1, axis=0))
    nxt = jnp.where(row == L - 1, 0.0, pltpu.roll(u, L - 1, axis=0))
    return w[0:1] * prev + w[1:2] * u + w[2:3] * nxt + b


def _hyena_prep_kernel(u0, u1, u2, w0, w1, w2, b0, b1, b2, x0_out, z_out, zb_out):
    x0_out[...] = _short_conv(u0[...], w0[...], b0[...])
    z = _short_conv(u1[...], w1[...], b1[...]) * _short_conv(u2[...], w2[...], b2[...])
    z_out[...] = z
    zb_out[...] = z.astype(BF16)


def _hyena_prep(proj, conv_w, conv_b, B, L):
    nc = HYENA_WIDTH // HY_COLS
    base = HY_OFF // HY_COLS
    u_spec = lambda part: pl.BlockSpec((L, HY_COLS), lambda b, c: (b, base + part * nc + c))
    w_spec = lambda part: pl.BlockSpec((3, HY_COLS), lambda b, c: (0, part * nc + c))
    b_spec = lambda part: pl.BlockSpec((1, HY_COLS), lambda b, c: (0, part * nc + c))
    o_spec = pl.BlockSpec((L, HY_COLS), lambda b, c: (b, c))
    shp = lambda dt: jax.ShapeDtypeStruct((B * L, HYENA_WIDTH), dt)
    return pl.pallas_call(
        _hyena_prep_kernel, grid=(B, nc),
        in_specs=[u_spec(0), u_spec(1), u_spec(2), w_spec(0), w_spec(1), w_spec(2), b_spec(0), b_spec(1), b_spec(2)],
        out_specs=(o_spec, o_spec, o_spec), out_shape=(shp(F32), shp(F32), shp(BF16)),
        compiler_params=_cparams("parallel", "parallel"), name="hyena_prep",
    )(proj, proj, proj, conv_w, conv_w, conv_w, conv_b, conv_b, conv_b)


def _filter_mlp_kernel(z_ref, w1, b1, f1, w2, b2, f2, w3, b3, dec_ref, o_ref):
    h = jnp.sin(f1[...] * (jnp.dot(z_ref[...], w1[...], preferred_element_type=F32) + b1[...]))
    h = jnp.sin(f2[...] * (jnp.dot(h, w2[...], preferred_element_type=F32) + b2[...]))
    h = jnp.dot(h, w3[...], preferred_element_type=F32) + b3[...]
    dec = dec_ref[...]
    o_ref[...] = h * jnp.concatenate([dec, dec], axis=1)


def _filter_mlp(z, w1, b1, f1, w2, b2, f2, w3, b3, decay):
    L = z.shape[0]
    tl = min(512, L)
    fixed = lambda a: pl.BlockSpec(a.shape, lambda i: (0, 0))
    return pl.pallas_call(
        _filter_mlp_kernel, grid=(L // tl,),
        in_specs=[pl.BlockSpec((tl, LANES), lambda i: (i, 0))] + [fixed(a) for a in (w1, b1, f1, w2, b2, f2, w3, b3)]
        + [pl.BlockSpec((tl, HYENA_WIDTH), lambda i: (i, 0))],
        out_specs=pl.BlockSpec((tl, 2 * HYENA_WIDTH), lambda i: (i, 0)),
        out_shape=jax.ShapeDtypeStruct((L, 2 * HYENA_WIDTH), F32),
        compiler_params=_cparams("parallel"), name="filter_mlp")(z, w1, b1, f1, w2, b2, f2, w3, b3, decay)


def _split_bf16(a):
    hi = a.astype(BF16)
    return hi, (a - hi.astype(F32)).astype(BF16)


def _filter_dft_kernel(fre, fim, k0_ref, k1_ref, o_re, o_im, *, n_fft):
    ft = fre.shape[0]
    k = pl.program_id(0) * ft + lax.broadcasted_iota(jnp.int32, (ft, 1), 0)
    sign = (1 - 2 * (k & 1)).astype(F32)
    amp = jnp.where(k == 0, 1.0 / n_fft, 2.0 / n_fft)
    k0h, k0l = _split_bf16(k0_ref[...])
    k1h, k1l = _split_bf16(k1_ref[...])

    def part(f):
        d = lambda a: jnp.dot(f, a, preferred_element_type=F32)
        return amp * ((d(k0h) + d(k0l)) + sign * (d(k1h) + d(k1l)))

    o_re[...] = part(fre[...])
    o_im[...] = part(fim[...])


def _filter_dft(fmat, k0, k1):
    L = k0.shape[0]
    ft = min(FREQ_TILE, L)
    nf = L // ft
    half = pl.BlockSpec((L, HYENA_WIDTH), lambda f: (0, 0))
    o_spec = pl.BlockSpec((ft, HYENA_WIDTH), lambda f: (f, 0))
    shp = jax.ShapeDtypeStruct((L, HYENA_WIDTH), F32)
    return pl.pallas_call(
        functools.partial(_filter_dft_kernel, n_fft=2 * L), grid=(nf,),
        in_specs=[pl.BlockSpec((ft, L), lambda f: (f, 0)), pl.BlockSpec((ft, L), lambda f: (nf + f, 0)), half, half],
        out_specs=(o_spec, o_spec), out_shape=(shp, shp),
        compiler_params=_cparams("parallel"), name="filter_dft")(fmat, fmat, k0, k1)


def _long_conv_kernel(z_ref, fre, fim, gre, gim, kre_ref, kim_ref, y_ref):
    f = pl.program_id(1)
    z = z_ref[...]
    zre = jnp.dot(fre[...], z, preferred_element_type=F32)
    zim = jnp.dot(fim[...], z, preferred_element_type=F32)
    kre, kim = kre_ref[...], kim_ref[...]
    a, b, c, d = zre * kre, zim * kim, zre * kim, zim * kre
    real_row = (lax.broadcasted_iota(jnp.int32, zre.shape, 0) == 0) & (f == 0)
    pre = jnp.where(real_row, a, a - b).astype(BF16)
    pim = jnp.where(real_row, b, c + d).astype(BF16)
    y = jnp.dot(gre[...], pre, preferred_element_type=F32) + jnp.dot(gim[...], pim, preferred_element_type=F32)

    @pl.when(f == 0)
    def _():
        y_ref[...] = y

    @pl.when(f > 0)
    def _():
        y_ref[...] += y


def _long_conv(zb, fmat, gmat, kre, kim, B, L):
    ft = min(FREQ_TILE, L)
    nf = L // ft
    k_spec = pl.BlockSpec((ft, HYENA_WIDTH), lambda b, f: (f, 0))
    return pl.pallas_call(
        _long_conv_kernel, grid=(B, nf),
        in_specs=[pl.BlockSpec((L, HYENA_WIDTH), lambda b, f: (b, 0)),
                  pl.BlockSpec((ft, L), lambda b, f: (f, 0)), pl.BlockSpec((ft, L), lambda b, f: (nf + f, 0)),
                  pl.BlockSpec((L, ft), lambda b, f: (0, f)), pl.BlockSpec((L, ft), lambda b, f: (0, nf + f)),
                  k_spec, k_spec],
        out_specs=pl.BlockSpec((L, HYENA_WIDTH), lambda b, f: (b, 0)),
        out_shape=jax.ShapeDtypeStruct((B * L, HYENA_WIDTH), F32),
        compiler_params=_cparams("parallel", "arbitrary"), name="long_conv")(zb, fmat, fmat, gmat, gmat, kre, kim)


def _dft_matrices(L):
    n = 2 * L

    def build(k, t):
        ang = ((k * t) % n).astype(F32) * (2.0 * math.pi / n)
        re = jnp.cos(ang)
        im = jnp.where(k == 0, (1 - 2 * (t & 1)).astype(F32), -jnp.sin(ang))
        return re.astype(BF16), im.astype(BF16)

    col = jnp.arange(L, dtype=jnp.int32)[:, None]
    row = jnp.arange(L, dtype=jnp.int32)[None, :]
    fmat = jnp.concatenate(build(col, row), axis=0)
    return fmat, fmat.T


def _filter_inputs(L):
    t = jnp.linspace(0.0, 1.0, L, dtype=F32)[:, None]
    bands = (FILTER_EMB_DIM - 1) // 2
    w = 2.0 * math.pi * jnp.arange(L, dtype=F32)[:, None] / L
    f = jnp.linspace(1e-4, bands - 1, bands, dtype=F32)[None, :]
    z = jnp.concatenate([t, jnp.cos(f * w), -jnp.sin(f * w)], axis=-1)
    deltas = jnp.linspace(MIN_DECAY, MAX_DECAY, HYENA_WIDTH, dtype=F32)
    decay = jnp.exp(-t * jnp.abs(deltas)[None, :])
    return jnp.pad(z, ((0, 0), (0, LANES - FILTER_EMB_DIM))), decay


def _pad_to(a, rows, cols):
    return jnp.pad(a, ((0, rows - a.shape[0]), (0, cols - a.shape[1])))


def _hyena_filter_spectrum(L, z_emb, decay, fmat, w1, b1, fr1, w2, b2, fr2, w3, b3):
    h = _filter_mlp(z_emb, _pad_to(w1, LANES, LANES), _pad_to(b1[None], 1, LANES), _pad_to(fr1[None], 1, LANES),
                    _pad_to(w2, LANES, LANES), _pad_to(b2[None], 1, LANES), _pad_to(fr2[None], 1, LANES),
                    _pad_to(w3, LANES, 2 * HYENA_WIDTH), b3[None], decay)
    h_f, h_b = h[:, :HYENA_WIDTH], h[:, HYENA_WIDTH:]
    k0 = jnp.concatenate([h_f[:1] + h_b[:1], h_f[1:]], axis=0)
    k1 = jnp.concatenate([jnp.zeros((1, HYENA_WIDTH), F32), h_b[:0:-1]], axis=0)
    return _filter_dft(fmat, k0, k1)


def _merge_kernel(x_ref, a_ref, x0_ref, z_ref, y_ref, g_ref, skip_ref, wab, whb, wout, nf_ref, wq, h_out, hn_out,
                  q_out):
    ya = jnp.dot(a_ref[...], wab[...], preferred_element_type=F32)
    hy = x0_ref[...] * (y_ref[...] + z_ref[...] * skip_ref[...])
    yh = jnp.dot(hy.astype(BF16), whb[...], preferred_element_type=F32)
    g = g_ref[...]
    merged = jax.nn.sigmoid(g[:, :D_MODEL]) * ya + jax.nn.sigmoid(g[:, D_MODEL:]) * yh
    h = x_ref[...] + jnp.dot(merged.astype(BF16), wout[...], preferred_element_type=F32)
    h_out[...] = h
    hn = _rms(h, nf_ref[...])
    hn_out[...] = hn
    q_out[...] = jnp.dot(hn.astype(BF16), wq[...], preferred_element_type=F32)


def _merge(x, attn, x0, z, y, proj, skip, wab, whb, wout, nf, wq):
    T = x.shape[0]
    tm = min(ROW_TILE, T)
    row = lambda w: pl.BlockSpec((tm, w), lambda i: (i, 0))
    fixed = lambda a: pl.BlockSpec(a.shape, lambda i: (0, 0))
    qw = wq.shape[1]
    return pl.pallas_call(
        _merge_kernel, grid=(T // tm,),
        in_specs=[row(D_MODEL), row(ATTN_WIDTH), row(HYENA_WIDTH), row(HYENA_WIDTH), row(HYENA_WIDTH),
                  row(2 * D_MODEL), fixed(skip), fixed(wab), fixed(whb), fixed(wout), fixed(nf), fixed(wq)],
        out_specs=(row(D_MODEL), row(D_MODEL), row(qw)),
        out_shape=(jax.ShapeDtypeStruct((T, D_MODEL), F32), jax.ShapeDtypeStruct((T, D_MODEL), F32),
                   jax.ShapeDtypeStruct((T, qw), F32)),
        compiler_params=_cparams("parallel"), name="merge_query")(x, attn, x0, z, y, proj, skip, wab, whb, wout, nf, wq)


NEG_INF = float("-inf")
POS_SENTINEL = 1e9


def _top_rows(s, pos, k):
    vals, sel = [], []
    for _ in range(k):
        m = jnp.max(s, axis=0, keepdims=True)
        p = jnp.min(jnp.where(s == m, pos, POS_SENTINEL), axis=0, keepdims=True)
        vals.append(m)
        sel.append(p)
        s = jnp.where(pos == p, NEG_INF, s)
    return vals, sel


def _peer_topk_kernel(q_ref, keys_ref, idx_out, gate_out):
    c = q_ref.shape[0]
    kk = PEER_TOPK
    row_id = lambda n: lax.broadcasted_iota(jnp.int32, (n, c), 0).astype(F32)
    key_pos = row_id(N_KEYS)
    a_small = kk // 2
    b_iota = row_id(a_small)
    idx_rows, gate_rows = [], []
    for h in range(PEER_HEADS):
        tops = []
        for p in range(2):
            qhp = q_ref[:, (2 * h + p) * HALF_QUERY:(2 * h + p + 1) * HALF_QUERY].astype(BF16)
            s = lax.dot_general(keys_ref[p], qhp, (((1,), (1,)), ((), ())), preferred_element_type=F32)
            vals, sel = _top_rows(s, key_pos, kk)
            tops.append((jnp.concatenate(vals, axis=0), jnp.concatenate(sel, axis=0)))
        (s0, i0), (s1, i1) = tops
        cand = [s0[0:1] + s1]
        eidx = [i0[0:1] * N_KEYS + i1]
        cpos = [row_id(kk)]
        for a in range(1, a_small):
            keep = (a + 1) * (b_iota + 1) <= kk
            cand.append(jnp.where(keep, s0[a:a + 1] + s1[:a_small], NEG_INF))
            eidx.append(i0[a:a + 1] * N_KEYS + i1[:a_small])
            cpos.append(a * kk + b_iota)
        cand.append(s0[a_small:] + s1[0:1])
        eidx.append(i0[a_small:] * N_KEYS + i1[0:1])
        cpos.append((b_iota + a_small) * kk)
        cand = jnp.concatenate(cand, axis=0)
        eidx = jnp.concatenate(eidx, axis=0)
        cpos = jnp.concatenate(cpos, axis=0)
        best, sel = _top_rows(cand, cpos, kk)
        chosen = [jnp.max(jnp.where(cpos == p_, eidx, -1.0), axis=0, keepdims=True) for p_ in sel]
        best = jnp.concatenate(best, axis=0)
        e = jnp.exp(best - best[0:1])
        gate_rows.append(e / jnp.sum(e, axis=0, keepdims=True))
        idx_rows.append(jnp.concatenate(chosen, axis=0))
    idx = jnp.concatenate(idx_rows, axis=0)
    gates = jnp.concatenate(gate_rows, axis=0)
    idx_out[...] = idx.T.astype(jnp.int32)
    gate_out[...] = gates.T


def _peer_topk(q, keys):
    T = q.shape[0]
    c = min(TOPK_TILE, T)
    o_spec = pl.BlockSpec((c, N_SLOTS), lambda i: (i, 0))
    return pl.pallas_call(
        _peer_topk_kernel, grid=(T // c,),
        in_specs=[pl.BlockSpec((c, q.shape[1]), lambda i: (i, 0)), pl.BlockSpec(keys.shape, lambda i: (0, 0, 0))],
        out_specs=(o_spec, o_spec),
        out_shape=(jax.ShapeDtypeStruct((T, N_SLOTS), jnp.int32), jax.ShapeDtypeStruct((T, N_SLOTS), F32)),
        compiler_params=_cparams("parallel"), name="peer_topk")(q, keys)


HI_MASK = 0xFFFF0000
IDX_GROUP = 8


def _unpack(words):
    lo = lax.bitcast_convert_type(words << 16, F32)
    hi = lax.bitcast_convert_type(words & jnp.uint32(HI_MASK), F32)
    return lo, hi


def _pack_table(t):
    b = lax.bitcast_convert_type(t.astype(BF16), jnp.uint16).astype(jnp.uint32)
    words = b[:, :HALF_D] | (b[:, HALF_D:] << 16)
    return words.reshape(t.shape[0], PACK_ROWS, LANES)


def _gather_rows(idx_ref, tab_ref, tt, stage):
    for g in range(N_SLOTS // IDX_GROUP):
        sub = idx_ref.at[0, 0, pl.ds(tt * N_SLOTS + g * IDX_GROUP, IDX_GROUP)]
        for j in range(IDX_GROUP):
            stage[pl.ds(PACK_ROWS * (g * IDX_GROUP + j), PACK_ROWS), :] = tab_ref[sub[j]]


def _staged_chunks(stage):
    for s in range(PACK_ROWS):
        lo, hi = _unpack(stage[pl.ds(s, N_SLOTS, stride=PACK_ROWS), :])
        yield s, lo, hi


def _two_token_pipeline(n_tokens, gather, compute, stages):
    last = n_tokens - 1
    gather(0, stages[0])

    def pair(i, carry):
        t0 = 2 * i
        gather(t0 + 1, stages[1])
        compute(t0, stages[0])
        gather(jnp.minimum(t0 + 2, last), stages[0])
        compute(t0 + 1, stages[1])
        return carry

    lax.fori_loop(0, n_tokens // 2, pair, 0)


def _expert_act_kernel(idx_ref, hn_ref, gate_ref, tab_ref, w_out, stage_a, stage_b):
    def compute(tt, stage):
        x = hn_ref[tt]
        acc = jnp.zeros((N_SLOTS, LANES), F32)
        for s, lo, hi in _staged_chunks(stage):
            acc = acc + lo * x[s:s + 1] + hi * x[PACK_ROWS + s:PACK_ROWS + s + 1]
        act = jnp.sum(acc.T, axis=0, keepdims=True)
        gelu = 0.5 * act * (1.0 + lax.erf(act * (2.0 ** -0.5)))
        w_out[tt] = gate_ref[tt] * gelu

    _two_token_pipeline(hn_ref.shape[0], functools.partial(_gather_rows, idx_ref, tab_ref), compute,
                        (stage_a, stage_b))


def _stage_scratch():
    return pltpu.VMEM((N_SLOTS * PACK_ROWS, LANES), jnp.uint32)


def _expert_act(idx3, hn3, gate3, table):
    T = hn3.shape[0]
    tb = idx3.shape[2] // N_SLOTS
    return pl.pallas_call(
        _expert_act_kernel, grid=(T // tb,),
        in_specs=[pl.BlockSpec((1, 1, tb * N_SLOTS), lambda i: (i, 0, 0), memory_space=pltpu.SMEM),
                  pl.BlockSpec((tb, SUBLANES, LANES), lambda i: (i, 0, 0)),
                  pl.BlockSpec((tb, 1, N_SLOTS), lambda i: (i, 0, 0)),
                  pl.BlockSpec(memory_space=pltpu.VMEM)],
        out_specs=pl.BlockSpec((tb, 1, N_SLOTS), lambda i: (i, 0, 0)),
        out_shape=jax.ShapeDtypeStruct((T, 1, N_SLOTS), F32),
        scratch_shapes=[_stage_scratch(), _stage_scratch()],
        compiler_params=_cparams("arbitrary"), name="expert_act")(idx3, hn3, gate3, table)


def _expert_sum_kernel(idx_ref, w_ref, tab_ref, o_ref, stage_a, stage_b):
    def compute(tt, stage):
        w_col = jnp.broadcast_to(w_ref[tt], (LANES, N_SLOTS)).T
        lo_rows, hi_rows = [], []
        for _, lo, hi in _staged_chunks(stage):
            lo_rows.append(jnp.sum(w_col * lo, axis=0, keepdims=True))
            hi_rows.append(jnp.sum(w_col * hi, axis=0, keepdims=True))
        o_ref[tt] = jnp.concatenate(lo_rows + hi_rows, axis=0)

    _two_token_pipeline(o_ref.shape[0], functools.partial(_gather_rows, idx_ref, tab_ref), compute,
                        (stage_a, stage_b))


def _expert_sum(idx3, w3, table):
    T = w3.shape[0]
    tb = idx3.shape[2] // N_SLOTS
    return pl.pallas_call(
        _expert_sum_kernel, grid=(T // tb,),
        in_specs=[pl.BlockSpec((1, 1, tb * N_SLOTS), lambda i: (i, 0, 0), memory_space=pltpu.SMEM),
                  pl.BlockSpec((tb, 1, N_SLOTS), lambda i: (i, 0, 0)),
                  pl.BlockSpec(memory_space=pltpu.VMEM)],
        out_specs=pl.BlockSpec((tb, SUBLANES, LANES), lambda i: (i, 0, 0)),
        out_shape=jax.ShapeDtypeStruct((T, SUBLANES, LANES), F32),
        scratch_shapes=[_stage_scratch(), _stage_scratch()],
        compiler_params=_cparams("arbitrary"), name="expert_sum")(idx3, w3, table)


def _final_kernel(h_ref, p_ref, g_ref, o_ref):
    o_ref[...] = _rms(h_ref[...] + p_ref[...], g_ref[...])


def _final_norm(h, p, g):
    T = h.shape[0]
    tm = min(ROW_TILE, T)
    row = pl.BlockSpec((tm, D_MODEL), lambda i: (i, 0))
    return pl.pallas_call(
        _final_kernel, grid=(T // tm,), in_specs=[row, row, pl.BlockSpec((1, D_MODEL), lambda i: (0, 0))],
        out_specs=row, out_shape=jax.ShapeDtypeStruct((T, D_MODEL), F32),
        compiler_params=_cparams("parallel"), name="final_norm")(h, p, g)


def _rope_tables(positions):
    half = ROPE_DIM // 2
    inv_freq = ROPE_THETA ** (-jnp.arange(0, ROPE_DIM, 2, dtype=F32) / ROPE_DIM)
    ang = positions.reshape(-1).astype(F32)[:, None] * inv_freq
    cos, sin = jnp.cos(ang), jnp.sin(ang)
    T = cos.shape[0]
    pad = HEAD_DIM - ROPE_DIM
    head = lambda first, second, fill: jnp.concatenate(
        [first, second, jnp.full((T, pad), fill, F32)], axis=1)
    zeros = jnp.zeros_like(sin)
    reps = LANES // HEAD_DIM
    return tuple(jnp.tile(head(*parts), (1, reps))
                 for parts in ((cos, cos, 1.0), (-sin, zeros, 0.0), (zeros, sin, 0.0)))


def _reorder_in_proj(w):
    a0 = ATTN_WIDTH
    a1, a2 = a0 + KV_WIDTH, a0 + 2 * KV_WIDTH
    a3 = a2 + 3 * HYENA_WIDTH
    q, k, v, hy, gates = w[:, :a0], w[:, a0:a1], w[:, a1:a2], w[:, a2:a3], w[:, a3:]
    return jnp.concatenate([gates, hy, q, k, v], axis=1).astype(BF16)


def kernel(x, positions, norm_mix, w_in, conv_w, conv_b, filt_w1, filt_b1, filt_freq1, filt_w2, filt_b2, filt_freq2,
           filt_w3, filt_b3, hyena_skip, attn_sink, w_attn_branch, w_hyena_branch, w_out, norm_ffn, w_query,
           sub_keys, expert_u, expert_v, norm_final):
    B, S, D = x.shape
    assert D == D_MODEL and S % BLOCK == 0
    T = B * S
    depth = w_in.shape[0]
    tb = min(GATHER_TOKENS, T)
    cos_t, s1_t, s2_t = _rope_tables(positions)
    fmat, gmat = _dft_matrices(S)
    z_emb, decay = _filter_inputs(S)

    h = x.reshape(T, D)
    peer = None
    for l in range(depth):
        res = _in_proj(h, peer, norm_mix[l][None], _reorder_in_proj(w_in[l]), cos_t, s1_t, s2_t)
        xcur, proj = (h, res) if peer is None else res
        attn = _attention(proj, attn_sink[l], B, S)
        x0, z, zb = _hyena_prep(proj, conv_w[l], conv_b[l][None], B, S)
        kre, kim = _hyena_filter_spectrum(S, z_emb, decay, fmat, filt_w1[l], filt_b1[l], filt_freq1[l], filt_w2[l],
                                          filt_b2[l], filt_freq2[l], filt_w3[l], filt_b3[l])
        ylong = _long_conv(zb, fmat, gmat, kre, kim, B, S)
        h, hn, q = _merge(xcur, attn, x0, z, ylong, proj, hyena_skip[l][None], w_attn_branch[l].astype(BF16),
                          w_hyena_branch[l].astype(BF16), w_out[l].astype(BF16), norm_ffn[l][None],
                          w_query[l].astype(BF16))
        idx, gates = _peer_topk(q, sub_keys[l].astype(BF16))
        idx3 = idx.reshape(T // tb, 1, tb * N_SLOTS)
        wts = _expert_act(idx3, hn.reshape(T, SUBLANES, LANES), gates.reshape(T, 1, N_SLOTS),
                          _pack_table(expert_u[l]))
        peer = _expert_sum(idx3, wts, _pack_table(expert_v[l])).reshape(T, D)
    return _final_norm(h, peer, norm_final[None]).reshape(B, S, D)
```

```python
import functools
import math

import jax
import jax.numpy as jnp
from jax import lax
from jax.experimental import pallas as pl
from jax.experimental.pallas import tpu as pltpu

F32 = jnp.float32
BF16 = jnp.bfloat16

D_MODEL = 1024
N_Q_HEADS = 8
N_KV_HEADS = 2
HEAD_DIM = 64
GQA_GROUP = N_Q_HEADS // N_KV_HEADS
ATTN_WIDTH = N_Q_HEADS * HEAD_DIM
KV_WIDTH = N_KV_HEADS * HEAD_DIM
WINDOW = 128
BLOCK = 128
ROPE_DIM = HEAD_DIM // 4
ROPE_THETA = 500000.0
HYENA_WIDTH = 512
FILTER_EMB_DIM = 33
FILTER_HIDDEN = 64
DECAY_TARGET = 1e-2
MAX_DECAY = math.log(DECAY_TARGET) / 0.3
MIN_DECAY = math.log(DECAY_TARGET) / 1.5
N_KEYS = 128
N_EXPERTS = N_KEYS * N_KEYS
PEER_HEADS = 8
PEER_TOPK = 16
HALF_QUERY = 128
N_SLOTS = PEER_HEADS * PEER_TOPK
EPS = 1e-6

LANES = 128
SUBLANES = 8
VMEM_LIMIT_BYTES = 56 * 1024 * 1024

GATE_OFF = 0
HY_OFF = 2 * D_MODEL
Q_OFF = HY_OFF + 3 * HYENA_WIDTH
K_OFF = Q_OFF + ATTN_WIDTH
V_OFF = K_OFF + KV_WIDTH
IN_WIDTH = V_OFF + KV_WIDTH
ROPE_WIDTH = ATTN_WIDTH + KV_WIDTH

ROW_TILE = 256
FREQ_TILE = 512
TOPK_TILE = 256
GATHER_TOKENS = 64
HALF_D = D_MODEL // 2
PACK_ROWS = HALF_D // LANES


def _cparams(*sem):
    return pltpu.CompilerParams(dimension_semantics=sem, vmem_limit_bytes=VMEM_LIMIT_BYTES)


def _rms(x, g):
    r = lax.rsqrt(jnp.mean(x * x, axis=-1, keepdims=True) + EPS)
    return x * r * g


def _in_proj_kernel(*refs, has_add):
    if has_add:
        h_ref, p_ref, g_ref, w_ref, c_ref, s1_ref, s2_ref, x_out, proj_out = refs
        x = h_ref[...] + p_ref[...]
        x_out[...] = x
    else:
        h_ref, g_ref, w_ref, c_ref, s1_ref, s2_ref, proj_out = refs
        x = h_ref[...]
    xn = _rms(x, g_ref[...]).astype(BF16)
    proj_out[:, :Q_OFF] = jnp.dot(xn, w_ref[:, :Q_OFF], preferred_element_type=F32)
    qk = jnp.dot(xn, w_ref[:, Q_OFF:V_OFF], preferred_element_type=F32)
    reps = ROPE_WIDTH // LANES
    cos = jnp.concatenate([c_ref[...]] * reps, axis=1)
    s1 = jnp.concatenate([s1_ref[...]] * reps, axis=1)
    s2 = jnp.concatenate([s2_ref[...]] * reps, axis=1)
    half = ROPE_DIM // 2
    rot = qk * cos + pltpu.roll(qk, ROPE_WIDTH - half, axis=1) * s1 + pltpu.roll(qk, half, axis=1) * s2
    proj_out[:, Q_OFF:V_OFF] = rot
    proj_out[:, V_OFF:] = jnp.dot(xn, w_ref[:, V_OFF:], preferred_element_type=F32)


def _in_proj(h, p, g, w, cos_t, s1_t, s2_t):
    T = h.shape[0]
    tm = min(ROW_TILE, T)
    has_add = p is not None
    row = lambda i: (i, 0)
    fixed = lambda i: (0, 0)
    x_spec = pl.BlockSpec((tm, D_MODEL), row)
    tab_spec = pl.BlockSpec((tm, LANES), row)
    in_specs = [x_spec] + ([x_spec] if has_add else []) + [
        pl.BlockSpec((1, D_MODEL), fixed), pl.BlockSpec((D_MODEL, IN_WIDTH), fixed), tab_spec, tab_spec, tab_spec]
    proj_shape = jax.ShapeDtypeStruct((T, IN_WIDTH), F32)
    proj_spec = pl.BlockSpec((tm, IN_WIDTH), row)
    if has_add:
        out_shape, out_specs = (jax.ShapeDtypeStruct((T, D_MODEL), F32), proj_shape), (x_spec, proj_spec)
        args = (h, p, g, w, cos_t, s1_t, s2_t)
    else:
        out_shape, out_specs = proj_shape, proj_spec
        args = (h, g, w, cos_t, s1_t, s2_t)
    return pl.pallas_call(
        functools.partial(_in_proj_kernel, has_add=has_add), grid=(T // tm,), in_specs=in_specs,
        out_specs=out_specs, out_shape=out_shape, compiler_params=_cparams("parallel"), name="in_proj")(*args)


ATTN_Q_BLOCKS = 2


def _attn_kernel(sink_ref, q_ref, kp_ref, ko_ref, kn_ref, o_ref, *, n_steps):
    i = pl.program_id(1)
    kv = jnp.concatenate([kp_ref[...], ko_ref[...], kn_ref[...]], axis=0)
    c = lax.broadcasted_iota(jnp.int32, (3 * BLOCK, BLOCK), 0)
    r = lax.broadcasted_iota(jnp.int32, (3 * BLOCK, BLOCK), 1)
    d = c - r
    band = (d >= BLOCK - WINDOW) & (d <= BLOCK + WINDOW)
    scale = HEAD_DIM ** -0.5
    for u in range(ATTN_Q_BLOCKS):
        valid = band
        if u == 0:
            valid = valid & ((c >= BLOCK) | (i > 0))
        if u == ATTN_Q_BLOCKS - 1:
            valid = valid & ((c < 2 * BLOCK) | (i < n_steps - 1))
        valid = jnp.concatenate([valid] * GQA_GROUP, axis=1)
        q = q_ref[u * BLOCK:(u + 1) * BLOCK, :] * scale
        kv_u = kv[u * BLOCK:(u + 3) * BLOCK]
        v_t = kv_u[:, KV_WIDTH:].T.astype(BF16)
        for g in range(N_KV_HEADS):
            k = kv_u[:, g * HEAD_DIM:(g + 1) * HEAD_DIM].astype(BF16)
            heads = [g * GQA_GROUP + hh for hh in range(GQA_GROUP)]
            qs = jnp.concatenate([q[:, h * HEAD_DIM:(h + 1) * HEAD_DIM] for h in heads], axis=0).astype(BF16)
            s = lax.dot_general(k, qs, (((1,), (1,)), ((), ())), preferred_element_type=F32)
            s = jnp.where(valid, s, -1e30)
            sink = jnp.concatenate([jnp.full((1, BLOCK), sink_ref[h], F32) for h in heads], axis=1)
            m = jnp.maximum(jnp.max(s, axis=0, keepdims=True), sink)
            p = jnp.exp(s - m)
            denom = jnp.sum(p, axis=0, keepdims=True) + jnp.exp(sink - m)
            o_t = jnp.dot(v_t[g * HEAD_DIM:(g + 1) * HEAD_DIM], p.astype(BF16), preferred_element_type=F32)
            o_t = o_t * (1.0 / denom)
            for hh in range(0, GQA_GROUP, 2):
                pair = jnp.concatenate([o_t[:, hh * BLOCK:(hh + 1) * BLOCK],
                                        o_t[:, (hh + 1) * BLOCK:(hh + 2) * BLOCK]], axis=0).T
                col = heads[hh] * HEAD_DIM
                o_ref[u * BLOCK:(u + 1) * BLOCK, col:col + 2 * HEAD_DIM] = pair.astype(o_ref.dtype)


def _attention(proj, sink, B, S):
    nb = S // BLOCK
    nq = ATTN_Q_BLOCKS
    assert nb % nq == 0
    ns = nb // nq
    qcol = Q_OFF // ATTN_WIDTH
    kvcol = K_OFF // (2 * KV_WIDTH)
    edge_spec = lambda f: pl.BlockSpec((BLOCK, 2 * KV_WIDTH), f)
    return pl.pallas_call(
        functools.partial(_attn_kernel, n_steps=ns), grid=(B, ns),
        in_specs=[pl.BlockSpec(memory_space=pltpu.SMEM),
                  pl.BlockSpec((nq * BLOCK, ATTN_WIDTH), lambda b, i: (b * ns + i, qcol)),
                  edge_spec(lambda b, i: (b * nb + jnp.maximum(i * nq - 1, 0), kvcol)),
                  pl.BlockSpec((nq * BLOCK, 2 * KV_WIDTH), lambda b, i: (b * ns + i, kvcol)),
                  edge_spec(lambda b, i: (b * nb + jnp.minimum((i + 1) * nq, nb - 1), kvcol))],
        out_specs=pl.BlockSpec((nq * BLOCK, ATTN_WIDTH), lambda b, i: (b * ns + i, 0)),
        out_shape=jax.ShapeDtypeStruct((B * S, ATTN_WIDTH), BF16),
        compiler_params=_cparams("parallel", "parallel"), name="window_attn")(sink, proj, proj, proj, proj)


HY_COLS = 256


def _short_conv(u, w, b):
    L = u.shape[0]
    row = lax.broadcasted_iota(jnp.int32, u.shape, 0)
    prev = jnp.where(row == 0, 0.0, pltpu.roll(u, 1, axis=0))
    nxt = jnp.where(row == L - 1, 0.0, pltpu.roll(u, L - 1, axis=0))
    return w[0:1] * prev + w[1:2] * u + w[2:3] * nxt + b


def _hyena_prep_kernel(u0, u1, u2, w0, w1, w2, b0, b1, b2, x0_out, z_out, zb_out):
    x0_out[...] = _short_conv(u0[...], w0[...], b0[...])
    z = _short_conv(u1[...], w1[...], b1[...]) * _short_conv(u2[...], w2[...], b2[...])
    z_out[...] = z
    zb_out[...] = z.astype(BF16)


def _hyena_prep(proj, conv_w, conv_b, B, L):
    nc = HYENA_WIDTH // HY_COLS
    base = HY_OFF // HY_COLS
    u_spec = lambda part: pl.BlockSpec((L, HY_COLS), lambda b, c: (b, base + part * nc + c))
    w_spec = lambda part: pl.BlockSpec((3, HY_COLS), lambda b, c: (0, part * nc + c))
    b_spec = lambda part: pl.BlockSpec((1, HY_COLS), lambda b, c: (0, part * nc + c))
    o_spec = pl.BlockSpec((L, HY_COLS), lambda b, c: (b, c))
    shp = lambda dt: jax.ShapeDtypeStruct((B * L, HYENA_WIDTH), dt)
    return pl.pallas_call(
        _hyena_prep_kernel, grid=(B, nc),
        in_specs=[u_spec(0), u_spec(1), u_spec(2), w_spec(0), w_spec(1), w_spec(2), b_spec(0), b_spec(1), b_spec(2)],
        out_specs=(o_spec, o_spec, o_spec), out_shape=(shp(F32), shp(F32), shp(BF16)),
        compiler_params=_cparams("parallel", "parallel"), name="hyena_prep",
    )(proj, proj, proj, conv_w, conv_w, conv_w, conv_b, conv_b, conv_b)


def _filter_mlp_kernel(z_ref, w1, b1, f1, w2, b2, f2, w3, b3, dec_ref, o_ref):
    h = jnp.sin(f1[...] * (jnp.dot(z_ref[...], w1[...], preferred_element_type=F32) + b1[...]))
    h = jnp.sin(f2[...] * (jnp.dot(h, w2[...], preferred_element_type=F32) + b2[...]))
    h = jnp.dot(h, w3[...], preferred_element_type=F32) + b3[...]
    dec = dec_ref[...]
    o_ref[...] = h * jnp.concatenate([dec, dec], axis=1)


def _filter_mlp(z, w1, b1, f1, w2, b2, f2, w3, b3, decay):
    L = z.shape[0]
    tl = min(512, L)
    fixed = lambda a: pl.BlockSpec(a.shape, lambda i: (0, 0))
    return pl.pallas_call(
        _filter_mlp_kernel, grid=(L // tl,),
        in_specs=[pl.BlockSpec((tl, LANES), lambda i: (i, 0))] + [fixed(a) for a in (w1, b1, f1, w2, b2, f2, w3, b3)]
        + [pl.BlockSpec((tl, HYENA_WIDTH), lambda i: (i, 0))],
        out_specs=pl.BlockSpec((tl, 2 * HYENA_WIDTH), lambda i: (i, 0)),
        out_shape=jax.ShapeDtypeStruct((L, 2 * HYENA_WIDTH), F32),
        compiler_params=_cparams("parallel"), name="filter_mlp")(z, w1, b1, f1, w2, b2, f2, w3, b3, decay)


def _split_bf16(a):
    hi = a.astype(BF16)
    return hi, (a - hi.astype(F32)).astype(BF16)


def _filter_dft_kernel(fre, fim, k0_ref, k1_ref, o_re, o_im, *, n_fft):
    ft = fre.shape[0]
    k = pl.program_id(0) * ft + lax.broadcasted_iota(jnp.int32, (ft, 1), 0)
    sign = (1 - 2 * (k & 1)).astype(F32)
    amp = jnp.where(k == 0, 1.0 / n_fft, 2.0 / n_fft)
    k0h, k0l = _split_bf16(k0_ref[...])
    k1h, k1l = _split_bf16(k1_ref[...])

    def part(f):
        d = lambda a: jnp.dot(f, a, preferred_element_type=F32)
        return amp * ((d(k0h) + d(k0l)) + sign * (d(k1h) + d(k1l)))

    o_re[...] = part(fre[...])
    o_im[...] = part(fim[...])


def _filter_dft(fmat, k0, k1):
    L = k0.shape[0]
    ft = min(FREQ_TILE, L)
    nf = L // ft
    half = pl.BlockSpec((L, HYENA_WIDTH), lambda f: (0, 0))
    o_spec = pl.BlockSpec((ft, HYENA_WIDTH), lambda f: (f, 0))
    shp = jax.ShapeDtypeStruct((L, HYENA_WIDTH), F32)
    return pl.pallas_call(
        functools.partial(_filter_dft_kernel, n_fft=2 * L), grid=(nf,),
        in_specs=[pl.BlockSpec((ft, L), lambda f: (f, 0)), pl.BlockSpec((ft, L), lambda f: (nf + f, 0)), half, half],
        out_specs=(o_spec, o_spec), out_shape=(shp, shp),
        compiler_params=_cparams("parallel"), name="filter_dft")(fmat, fmat, k0, k1)


def _long_conv_kernel(z_ref, fre, fim, gre, gim, kre_ref, kim_ref, y_ref):
    f = pl.program_id(1)
    z = z_ref[...]
    zre = jnp.dot(fre[...], z, preferred_element_type=F32)
    zim = jnp.dot(fim[...], z, preferred_element_type=F32)
    kre, kim = kre_ref[...], kim_ref[...]
    a, b, c, d = zre * kre, zim * kim, zre * kim, zim * kre
    real_row = (lax.broadcasted_iota(jnp.int32, zre.shape, 0) == 0) & (f == 0)
    pre = jnp.where(real_row, a, a - b).astype(BF16)
    pim = jnp.where(real_row, b, c + d).astype(BF16)
    y = jnp.dot(gre[...], pre, preferred_element_type=F32) + jnp.dot(gim[...], pim, preferred_element_type=F32)

    @pl.when(f == 0)
    def _():
        y_ref[...] = y

    @pl.when(f > 0)
    def _():
        y_ref[...] += y


def _long_conv(zb, fmat, gmat, kre, kim, B, L):
    ft = min(FREQ_TILE, L)
    nf = L // ft
    k_spec = pl.BlockSpec((ft, HYENA_WIDTH), lambda b, f: (f, 0))
    return pl.pallas_call(
        _long_conv_kernel, grid=(B, nf),
        in_specs=[pl.BlockSpec((L, HYENA_WIDTH), lambda b, f: (b, 0)),
                  pl.BlockSpec((ft, L), lambda b, f: (f, 0)), pl.BlockSpec((ft, L), lambda b, f: (nf + f, 0)),
                  pl.BlockSpec((L, ft), lambda b, f: (0, f)), pl.BlockSpec((L, ft), lambda b, f: (0, nf + f)),
                  k_spec, k_spec],
        out_specs=pl.BlockSpec((L, HYENA_WIDTH), lambda b, f: (b, 0)),
        out_shape=jax.ShapeDtypeStruct((B * L, HYENA_WIDTH), F32),
        compiler_params=_cparams("parallel", "arbitrary"), name="long_conv")(zb, fmat, fmat, gmat, gmat, kre, kim)


def _dft_matrices(L):
    n = 2 * L

    def build(k, t):
        ang = ((k * t) % n).astype(F32) * (2.0 * math.pi / n)
        re = jnp.cos(ang)
        im = jnp.where(k == 0, (1 - 2 * (t & 1)).astype(F32), -jnp.sin(ang))
        return re.astype(BF16), im.astype(BF16)

    col = jnp.arange(L, dtype=jnp.int32)[:, None]
    row = jnp.arange(L, dtype=jnp.int32)[None, :]
    fmat = jnp.concatenate(build(col, row), axis=0)
    return fmat, fmat.T


def _filter_inputs(L):
    t = jnp.linspace(0.0, 1.0, L, dtype=F32)[:, None]
    bands = (FILTER_EMB_DIM - 1) // 2
    w = 2.0 * math.pi * jnp.arange(L, dtype=F32)[:, None] / L
    f = jnp.linspace(1e-4, bands - 1, bands, dtype=F32)[None, :]
    z = jnp.concatenate([t, jnp.cos(f * w), -jnp.sin(f * w)], axis=-1)
    deltas = jnp.linspace(MIN_DECAY, MAX_DECAY, HYENA_WIDTH, dtype=F32)
    decay = jnp.exp(-t * jnp.abs(deltas)[None, :])
    return jnp.pad(z, ((0, 0), (0, LANES - FILTER_EMB_DIM))), decay


def _pad_to(a, rows, cols):
    return jnp.pad(a, ((0, rows - a.shape[0]), (0, cols - a.shape[1])))


def _hyena_filter_spectrum(L, z_emb, decay, fmat, w1, b1, fr1, w2, b2, fr2, w3, b3):
    h = _filter_mlp(z_emb, _pad_to(w1, LANES, LANES), _pad_to(b1[None], 1, LANES), _pad_to(fr1[None], 1, LANES),
                    _pad_to(w2, LANES, LANES), _pad_to(b2[None], 1, LANES), _pad_to(fr2[None], 1, LANES),
                    _pad_to(w3, LANES, 2 * HYENA_WIDTH), b3[None], decay)
    h_f, h_b = h[:, :HYENA_WIDTH], h[:, HYENA_WIDTH:]
    k0 = jnp.concatenate([h_f[:1] + h_b[:1], h_f[1:]], axis=0)
    k1 = jnp.concatenate([jnp.zeros((1, HYENA_WIDTH), F32), h_b[:0:-1]], axis=0)
    return _filter_dft(fmat, k0, k1)


def _merge_kernel(x_ref, a_ref, x0_ref, z_ref, y_ref, g_ref, skip_ref, wab, whb, wout, nf_ref, wq, h_out, hn_out,
                  q_out):
    ya = jnp.dot(a_ref[...], wab[...], preferred_element_type=F32)
    hy = x0_ref[...] * (y_ref[...] + z_ref[...] * skip_ref[...])
    yh = jnp.dot(hy.astype(BF16), whb[...], preferred_element_type=F32)
    g = g_ref[...]
    merged = jax.nn.sigmoid(g[:, :D_MODEL]) * ya + jax.nn.sigmoid(g[:, D_MODEL:]) * yh
    h = x_ref[...] + jnp.dot(merged.astype(BF16), wout[...], preferred_element_type=F32)
    h_out[...] = h
    hn = _rms(h, nf_ref[...])
    hn_out[...] = hn
    q_out[...] = jnp.dot(hn.astype(BF16), wq[...], preferred_element_type=F32)


def _merge(x, attn, x0, z, y, proj, skip, wab, whb, wout, nf, wq):
    T = x.shape[0]
    tm = min(ROW_TILE, T)
    row = lambda w: pl.BlockSpec((tm, w), lambda i: (i, 0))
    fixed = lambda a: pl.BlockSpec(a.shape, lambda i: (0, 0))
    qw = wq.shape[1]
    return pl.pallas_call(
        _merge_kernel, grid=(T // tm,),
        in_specs=[row(D_MODEL), row(ATTN_WIDTH), row(HYENA_WIDTH), row(HYENA_WIDTH), row(HYENA_WIDTH),
                  row(2 * D_MODEL), fixed(skip), fixed(wab), fixed(whb), fixed(wout), fixed(nf), fixed(wq)],
        out_specs=(row(D_MODEL), row(D_MODEL), row(qw)),
        out_shape=(jax.ShapeDtypeStruct((T, D_MODEL), F32), jax.ShapeDtypeStruct((T, D_MODEL), F32),
                   jax.ShapeDtypeStruct((T, qw), F32)),
        compiler_params=_cparams("parallel"), name="merge_query")(x, attn, x0, z, y, proj, skip, wab, whb, wout, nf, wq)


NEG_INF = float("-inf")
POS_SENTINEL = 1e9


def _top_rows(s, pos, k):
    vals, sel = [], []
    for _ in range(k):
        m = jnp.max(s, axis=0, keepdims=True)
        p = jnp.min(jnp.where(s == m, pos, POS_SENTINEL), axis=0, keepdims=True)
        vals.append(m)
        sel.append(p)
        s = jnp.where(pos == p, NEG_INF, s)
    return vals, sel


def _peer_topk_kernel(q_ref, keys_ref, idx_out, gate_out):
    c = q_ref.shape[0]
    kk = PEER_TOPK
    row_id = lambda n: lax.broadcasted_iota(jnp.int32, (n, c), 0).astype(F32)
    key_pos = row_id(N_KEYS)
    a_small = kk // 2
    b_iota = row_id(a_small)
    idx_rows, gate_rows = [], []
    for h in range(PEER_HEADS):
        tops = []
        for p in range(2):
            qhp = q_ref[:, (2 * h + p) * HALF_QUERY:(2 * h + p + 1) * HALF_QUERY].astype(BF16)
            s = lax.dot_general(keys_ref[p], qhp, (((1,), (1,)), ((), ())), preferred_element_type=F32)
            vals, sel = _top_rows(s, key_pos, kk)
            tops.append((jnp.concatenate(vals, axis=0), jnp.concatenate(sel, axis=0)))
        (s0, i0), (s1, i1) = tops
        cand = [s0[0:1] + s1]
        eidx = [i0[0:1] * N_KEYS + i1]
        cpos = [row_id(kk)]
        for a in range(1, a_small):
            keep = (a + 1) * (b_iota + 1) <= kk
            cand.append(jnp.where(keep, s0[a:a + 1] + s1[:a_small], NEG_INF))
            eidx.append(i0[a:a + 1] * N_KEYS + i1[:a_small])
            cpos.append(a * kk + b_iota)
        cand.append(s0[a_small:] + s1[0:1])
        eidx.append(i0[a_small:] * N_KEYS + i1[0:1])
        cpos.append((b_iota + a_small) * kk)
        cand = jnp.concatenate(cand, axis=0)
        eidx = jnp.concatenate(eidx, axis=0)
        cpos = jnp.concatenate(cpos, axis=0)
        best, sel = _top_rows(cand, cpos, kk)
        chosen = [jnp.max(jnp.where(cpos == p_, eidx, -1.0), axis=0, keepdims=True) for p_ in sel]
        best = jnp.concatenate(best, axis=0)
        e = jnp.exp(best - best[0:1])
        gate_rows.append(e / jnp.sum(e, axis=0, keepdims=True))
        idx_rows.append(jnp.concatenate(chosen, axis=0))
    idx = jnp.concatenate(idx_rows, axis=0)
    gates = jnp.concatenate(gate_rows, axis=0)
    idx_out[...] = (idx * PACK_ROWS).T.astype(jnp.int32)
    gate_out[...] = gates.T


def _peer_topk(q, keys):
    T = q.shape[0]
    c = min(TOPK_TILE, T)
    o_spec = pl.BlockSpec((c, N_SLOTS), lambda i: (i, 0))
    return pl.pallas_call(
        _peer_topk_kernel, grid=(T // c,),
        in_specs=[pl.BlockSpec((c, q.shape[1]), lambda i: (i, 0)), pl.BlockSpec(keys.shape, lambda i: (0, 0, 0))],
        out_specs=(o_spec, o_spec),
        out_shape=(jax.ShapeDtypeStruct((T, N_SLOTS), jnp.int32), jax.ShapeDtypeStruct((T, N_SLOTS), F32)),
        compiler_params=_cparams("parallel"), name="peer_topk")(q, keys)


HI_MASK = 0xFFFF0000
IDX_GROUP = 8


def _unpack(words):
    lo = lax.bitcast_convert_type(words << 16, F32)
    hi = lax.bitcast_convert_type(words & jnp.uint32(HI_MASK), F32)
    return lo, hi


PACK_TILE = 512


def _pack_kernel(t_ref, o_ref):
    t = t_ref[...]
    bf16_bits = lambda a: lax.bitcast_convert_type(a.astype(BF16).astype(F32), jnp.uint32)
    words = (bf16_bits(t[:, :HALF_D]) >> 16) | (bf16_bits(t[:, HALF_D:]) & jnp.uint32(HI_MASK))
    rows = t.shape[0]
    for s in range(PACK_ROWS):
        o_ref[pl.ds(s, rows, stride=PACK_ROWS), :] = words[:, s * LANES:(s + 1) * LANES]


def _pack_table(t):
    n = t.shape[0]
    rows = min(PACK_TILE, n)
    return pl.pallas_call(
        _pack_kernel, grid=(n // rows,), in_specs=[pl.BlockSpec((rows, D_MODEL), lambda i: (i, 0))],
        out_specs=pl.BlockSpec((rows * PACK_ROWS, LANES), lambda i: (i, 0)),
        out_shape=jax.ShapeDtypeStruct((n * PACK_ROWS, LANES), jnp.uint32),
        compiler_params=_cparams("parallel"), name="pack_table")(t)


def _gather_rows(idx_ref, tab_ref, tt, stage):
    for g in range(N_SLOTS // IDX_GROUP):
        sub = idx_ref.at[0, 0, pl.ds(tt * N_SLOTS + g * IDX_GROUP, IDX_GROUP)]
        for j in range(IDX_GROUP):
            row0 = pl.multiple_of(sub[j], PACK_ROWS)
            stage[pl.ds(PACK_ROWS * (g * IDX_GROUP + j), PACK_ROWS), :] = tab_ref[pl.ds(row0, PACK_ROWS), :]


def _staged_chunks(stage):
    for s in range(PACK_ROWS):
        lo, hi = _unpack(stage[pl.ds(s, N_SLOTS, stride=PACK_ROWS), :])
        yield s, lo, hi


def _two_token_pipeline(n_tokens, gather, compute, stages):
    last = n_tokens - 1
    gather(0, stages[0])

    def pair(i, carry):
        t0 = 2 * i
        gather(t0 + 1, stages[1])
        compute(t0, stages[0])
        gather(jnp.minimum(t0 + 2, last), stages[0])
        compute(t0 + 1, stages[1])
        return carry

    lax.fori_loop(0, n_tokens // 2, pair, 0)


def _expert_act_kernel(idx_ref, hn_ref, gate_ref, tab_ref, w_out, stage_a, stage_b):
    def compute(tt, stage):
        x = hn_ref[tt]
        acc = jnp.zeros((N_SLOTS, LANES), F32)
        for s, lo, hi in _staged_chunks(stage):
            acc = acc + lo * x[s:s + 1] + hi * x[PACK_ROWS + s:PACK_ROWS + s + 1]
        act = jnp.sum(acc.T, axis=0, keepdims=True)
        gelu = 0.5 * act * (1.0 + lax.erf(act * (2.0 ** -0.5)))
        w_out[tt] = gate_ref[tt] * gelu

    _two_token_pipeline(hn_ref.shape[0], functools.partial(_gather_rows, idx_ref, tab_ref), compute,
                        (stage_a, stage_b))


def _stage_scratch():
    return pltpu.VMEM((N_SLOTS * PACK_ROWS, LANES), jnp.uint32)


def _expert_act(idx3, hn3, gate3, table):
    T = hn3.shape[0]
    tb = idx3.shape[2] // N_SLOTS
    return pl.pallas_call(
        _expert_act_kernel, grid=(T // tb,),
        in_specs=[pl.BlockSpec((1, 1, tb * N_SLOTS), lambda i: (i, 0, 0), memory_space=pltpu.SMEM),
                  pl.BlockSpec((tb, SUBLANES, LANES), lambda i: (i, 0, 0)),
                  pl.BlockSpec((tb, 1, N_SLOTS), lambda i: (i, 0, 0)),
                  pl.BlockSpec(memory_space=pltpu.VMEM)],
        out_specs=pl.BlockSpec((tb, 1, N_SLOTS), lambda i: (i, 0, 0)),
        out_shape=jax.ShapeDtypeStruct((T, 1, N_SLOTS), F32),
        scratch_shapes=[_stage_scratch(), _stage_scratch()],
        compiler_params=_cparams("arbitrary"), name="expert_act")(idx3, hn3, gate3, table)


def _expert_sum_kernel(idx_ref, w_ref, tab_ref, o_ref, stage_a, stage_b):
    def compute(tt, stage):
        w_col = jnp.broadcast_to(w_ref[tt], (LANES, N_SLOTS)).T
        lo_rows, hi_rows = [], []
        for _, lo, hi in _staged_chunks(stage):
            lo_rows.append(jnp.sum(w_col * lo, axis=0, keepdims=True))
            hi_rows.append(jnp.sum(w_col * hi, axis=0, keepdims=True))
        o_ref[tt] = jnp.concatenate(lo_rows + hi_rows, axis=0)

    _two_token_pipeline(o_ref.shape[0], functools.partial(_gather_rows, idx_ref, tab_ref), compute,
                        (stage_a, stage_b))


def _expert_sum(idx3, w3, table):
    T = w3.shape[0]
    tb = idx3.shape[2] // N_SLOTS
    return pl.pallas_call(
        _expert_sum_kernel, grid=(T // tb,),
        in_specs=[pl.BlockSpec((1, 1, tb * N_SLOTS), lambda i: (i, 0, 0), memory_space=pltpu.SMEM),
                  pl.BlockSpec((tb, 1, N_SLOTS), lambda i: (i, 0, 0)),
                  pl.BlockSpec(memory_space=pltpu.VMEM)],
        out_specs=pl.BlockSpec((tb, SUBLANES, LANES), lambda i: (i, 0, 0)),
        out_shape=jax.ShapeDtypeStruct((T, SUBLANES, LANES), F32),
        scratch_shapes=[_stage_scratch(), _stage_scratch()],
        compiler_params=_cparams("arbitrary"), name="expert_sum")(idx3, w3, table)


def _final_kernel(h_ref, p_ref, g_ref, o_ref):
    o_ref[...] = _rms(h_ref[...] + p_ref[...], g_ref[...])


def _final_norm(h, p, g):
    T = h.shape[0]
    tm = min(ROW_TILE, T)
    row = pl.BlockSpec((tm, D_MODEL), lambda i: (i, 0))
    return pl.pallas_call(
        _final_kernel, grid=(T // tm,), in_specs=[row, row, pl.BlockSpec((1, D_MODEL), lambda i: (0, 0))],
        out_specs=row, out_shape=jax.ShapeDtypeStruct((T, D_MODEL), F32),
        compiler_params=_cparams("parallel"), name="final_norm")(h, p, g)


def _rope_tables(positions):
    half = ROPE_DIM // 2
    inv_freq = ROPE_THETA ** (-jnp.arange(0, ROPE_DIM, 2, dtype=F32) / ROPE_DIM)
    ang = positions.reshape(-1).astype(F32)[:, None] * inv_freq
    cos, sin = jnp.cos(ang), jnp.sin(ang)
    T = cos.shape[0]
    pad = HEAD_DIM - ROPE_DIM
    head = lambda first, second, fill: jnp.concatenate(
        [first, second, jnp.full((T, pad), fill, F32)], axis=1)
    zeros = jnp.zeros_like(sin)
    reps = LANES // HEAD_DIM
    return tuple(jnp.tile(head(*parts), (1, reps))
                 for parts in ((cos, cos, 1.0), (-sin, zeros, 0.0), (zeros, sin, 0.0)))


def _reorder_in_proj(w):
    a0 = ATTN_WIDTH
    a1, a2 = a0 + KV_WIDTH, a0 + 2 * KV_WIDTH
    a3 = a2 + 3 * HYENA_WIDTH
    q, k, v, hy, gates = w[:, :a0], w[:, a0:a1], w[:, a1:a2], w[:, a2:a3], w[:, a3:]
    return jnp.concatenate([gates, hy, q, k, v], axis=1).astype(BF16)


def kernel(x, positions, norm_mix, w_in, conv_w, conv_b, filt_w1, filt_b1, filt_freq1, filt_w2, filt_b2, filt_freq2,
           filt_w3, filt_b3, hyena_skip, attn_sink, w_attn_branch, w_hyena_branch, w_out, norm_ffn, w_query,
           sub_keys, expert_u, expert_v, norm_final):
    B, S, D = x.shape
    assert D == D_MODEL and S % BLOCK == 0
    T = B * S
    depth = w_in.shape[0]
    tb = min(GATHER_TOKENS, T)
    cos_t, s1_t, s2_t = _rope_tables(positions)
    fmat, gmat = _dft_matrices(S)
    z_emb, decay = _filter_inputs(S)

    h = x.reshape(T, D)
    peer = None
    for l in range(depth):
        res = _in_proj(h, peer, norm_mix[l][None], _reorder_in_proj(w_in[l]), cos_t, s1_t, s2_t)
        xcur, proj = (h, res) if peer is None else res
        attn = _attention(proj, attn_sink[l], B, S)
        x0, z, zb = _hyena_prep(proj, conv_w[l], conv_b[l][None], B, S)
        kre, kim = _hyena_filter_spectrum(S, z_emb, decay, fmat, filt_w1[l], filt_b1[l], filt_freq1[l], filt_w2[l],
                                          filt_b2[l], filt_freq2[l], filt_w3[l], filt_b3[l])
        ylong = _long_conv(zb, fmat, gmat, kre, kim, B, S)
        h, hn, q = _merge(xcur, attn, x0, z, ylong, proj, hyena_skip[l][None], w_attn_branch[l].astype(BF16),
                          w_hyena_branch[l].astype(BF16), w_out[l].astype(BF16), norm_ffn[l][None],
                          w_query[l].astype(BF16))
        idx, gates = _peer_topk(q, sub_keys[l].astype(BF16))
        idx3 = idx.reshape(T // tb, 1, tb * N_SLOTS)
        wts = _expert_act(idx3, hn.reshape(T, SUBLANES, LANES), gates.reshape(T, 1, N_SLOTS),
                          _pack_table(expert_u[l]))
        peer = _expert_sum(idx3, wts, _pack_table(expert_v[l])).reshape(T, D)
    return _final_norm(h, peer, norm_final[None]).reshape(B, S, D)
```

```python
import functools
import math

import jax
import jax.numpy as jnp
from jax import lax
from jax.experimental import pallas as pl
from jax.experimental.pallas import tpu as pltpu

F32 = jnp.float32
BF16 = jnp.bfloat16

D_MODEL = 1024
N_Q_HEADS = 8
N_KV_HEADS = 2
HEAD_DIM = 64
GQA_GROUP = N_Q_HEADS // N_KV_HEADS
ATTN_WIDTH = N_Q_HEADS * HEAD_DIM
KV_WIDTH = N_KV_HEADS * HEAD_DIM
WINDOW = 128
BLOCK = 128
ROPE_DIM = HEAD_DIM // 4
ROPE_THETA = 500000.0
HYENA_WIDTH = 512
FILTER_EMB_DIM = 33
FILTER_HIDDEN = 64
DECAY_TARGET = 1e-2
MAX_DECAY = math.log(DECAY_TARGET) / 0.3
MIN_DECAY = math.log(DECAY_TARGET) / 1.5
N_KEYS = 128
N_EXPERTS = N_KEYS * N_KEYS
PEER_HEADS = 8
PEER_TOPK = 16
HALF_QUERY = 128
N_SLOTS = PEER_HEADS * PEER_TOPK
EPS = 1e-6

LANES = 128
SUBLANES = 8
VMEM_LIMIT_BYTES = 56 * 1024 * 1024

GATE_OFF = 0
HY_OFF = 2 * D_MODEL
Q_OFF = HY_OFF + 3 * HYENA_WIDTH
K_OFF = Q_OFF + ATTN_WIDTH
V_OFF = K_OFF + KV_WIDTH
IN_WIDTH = V_OFF + KV_WIDTH
ROPE_WIDTH = ATTN_WIDTH + KV_WIDTH

ROW_TILE = 256
FREQ_TILE = 512
TOPK_TILE = 256
GATHER_TOKENS = 64
HALF_D = D_MODEL // 2
PACK_ROWS = HALF_D // LANES


def _cparams(*sem):
    return pltpu.CompilerParams(dimension_semantics=sem, vmem_limit_bytes=VMEM_LIMIT_BYTES)


def _load_rows3(ref):
    return jnp.concatenate([ref[:, s, :] for s in range(SUBLANES)], axis=1)


def _rms(x, g):
    r = lax.rsqrt(jnp.mean(x * x, axis=-1, keepdims=True) + EPS)
    return x * r * g


def _in_proj_kernel(*refs, has_add):
    if has_add:
        h_ref, p_ref, g_ref, w_ref, c_ref, s1_ref, s2_ref, x_out, proj_out = refs
        x = h_ref[...] + _load_rows3(p_ref)
        x_out[...] = x
    else:
        h_ref, g_ref, w_ref, c_ref, s1_ref, s2_ref, proj_out = refs
        x = h_ref[...]
    xn = _rms(x, g_ref[...]).astype(BF16)
    proj_out[:, :Q_OFF] = jnp.dot(xn, w_ref[:, :Q_OFF], preferred_element_type=F32)
    qk = jnp.dot(xn, w_ref[:, Q_OFF:V_OFF], preferred_element_type=F32)
    reps = ROPE_WIDTH // LANES
    cos = jnp.concatenate([c_ref[...]] * reps, axis=1)
    s1 = jnp.concatenate([s1_ref[...]] * reps, axis=1)
    s2 = jnp.concatenate([s2_ref[...]] * reps, axis=1)
    half = ROPE_DIM // 2
    rot = qk * cos + pltpu.roll(qk, ROPE_WIDTH - half, axis=1) * s1 + pltpu.roll(qk, half, axis=1) * s2
    proj_out[:, Q_OFF:V_OFF] = rot
    proj_out[:, V_OFF:] = jnp.dot(xn, w_ref[:, V_OFF:], preferred_element_type=F32)


def _in_proj(h, p, g, w, cos_t, s1_t, s2_t):
    T = h.shape[0]
    tm = min(ROW_TILE, T)
    has_add = p is not None
    row = lambda i: (i, 0)
    fixed = lambda i: (0, 0)
    x_spec = pl.BlockSpec((tm, D_MODEL), row)
    tab_spec = pl.BlockSpec((tm, LANES), row)
    in_specs = [x_spec] + ([pl.BlockSpec((tm, SUBLANES, LANES), lambda i: (i, 0, 0))] if has_add else []) + [
        pl.BlockSpec((1, D_MODEL), fixed), pl.BlockSpec((D_MODEL, IN_WIDTH), fixed), tab_spec, tab_spec, tab_spec]
    proj_shape = jax.ShapeDtypeStruct((T, IN_WIDTH), F32)
    proj_spec = pl.BlockSpec((tm, IN_WIDTH), row)
    if has_add:
        out_shape, out_specs = (jax.ShapeDtypeStruct((T, D_MODEL), F32), proj_shape), (x_spec, proj_spec)
        args = (h, p, g, w, cos_t, s1_t, s2_t)
    else:
        out_shape, out_specs = proj_shape, proj_spec
        args = (h, g, w, cos_t, s1_t, s2_t)
    return pl.pallas_call(
        functools.partial(_in_proj_kernel, has_add=has_add), grid=(T // tm,), in_specs=in_specs,
        out_specs=out_specs, out_shape=out_shape, compiler_params=_cparams("parallel"), name="in_proj")(*args)


ATTN_Q_BLOCKS = 2


def _attn_kernel(sink_ref, q_ref, kp_ref, ko_ref, kn_ref, o_ref, *, n_steps):
    i = pl.program_id(1)
    kv = jnp.concatenate([kp_ref[...], ko_ref[...], kn_ref[...]], axis=0)
    c = lax.broadcasted_iota(jnp.int32, (3 * BLOCK, BLOCK), 0)
    r = lax.broadcasted_iota(jnp.int32, (3 * BLOCK, BLOCK), 1)
    d = c - r
    band = (d >= BLOCK - WINDOW) & (d <= BLOCK + WINDOW)
    scale = HEAD_DIM ** -0.5
    for u in range(ATTN_Q_BLOCKS):
        valid = band
        if u == 0:
            valid = valid & ((c >= BLOCK) | (i > 0))
        if u == ATTN_Q_BLOCKS - 1:
            valid = valid & ((c < 2 * BLOCK) | (i < n_steps - 1))
        valid = jnp.concatenate([valid] * GQA_GROUP, axis=1)
        q = q_ref[u * BLOCK:(u + 1) * BLOCK, :] * scale
        kv_u = kv[u * BLOCK:(u + 3) * BLOCK]
        v_t = kv_u[:, KV_WIDTH:].T.astype(BF16)
        for g in range(N_KV_HEADS):
            k = kv_u[:, g * HEAD_DIM:(g + 1) * HEAD_DIM].astype(BF16)
            heads = [g * GQA_GROUP + hh for hh in range(GQA_GROUP)]
            qs = jnp.concatenate([q[:, h * HEAD_DIM:(h + 1) * HEAD_DIM] for h in heads], axis=0).astype(BF16)
            s = lax.dot_general(k, qs, (((1,), (1,)), ((), ())), preferred_element_type=F32)
            s = jnp.where(valid, s, -1e30)
            sink = jnp.concatenate([jnp.full((1, BLOCK), sink_ref[h], F32) for h in heads], axis=1)
            m = jnp.maximum(jnp.max(s, axis=0, keepdims=True), sink)
            p = jnp.exp(s - m)
            denom = jnp.sum(p, axis=0, keepdims=True) + jnp.exp(sink - m)
            o_t = jnp.dot(v_t[g * HEAD_DIM:(g + 1) * HEAD_DIM], p.astype(BF16), preferred_element_type=F32)
            o_t = o_t * (1.0 / denom)
            for hh in range(0, GQA_GROUP, 2):
                pair = jnp.concatenate([o_t[:, hh * BLOCK:(hh + 1) * BLOCK],
                                        o_t[:, (hh + 1) * BLOCK:(hh + 2) * BLOCK]], axis=0).T
                col = heads[hh] * HEAD_DIM
                o_ref[u * BLOCK:(u + 1) * BLOCK, col:col + 2 * HEAD_DIM] = pair.astype(o_ref.dtype)


def _attention(proj, sink, B, S):
    nb = S // BLOCK
    nq = ATTN_Q_BLOCKS
    assert nb % nq == 0
    ns = nb // nq
    qcol = Q_OFF // ATTN_WIDTH
    kvcol = K_OFF // (2 * KV_WIDTH)
    edge_spec = lambda f: pl.BlockSpec((BLOCK, 2 * KV_WIDTH), f)
    return pl.pallas_call(
        functools.partial(_attn_kernel, n_steps=ns), grid=(B, ns),
        in_specs=[pl.BlockSpec(memory_space=pltpu.SMEM),
                  pl.BlockSpec((nq * BLOCK, ATTN_WIDTH), lambda b, i: (b * ns + i, qcol)),
                  edge_spec(lambda b, i: (b * nb + jnp.maximum(i * nq - 1, 0), kvcol)),
                  pl.BlockSpec((nq * BLOCK, 2 * KV_WIDTH), lambda b, i: (b * ns + i, kvcol)),
                  edge_spec(lambda b, i: (b * nb + jnp.minimum((i + 1) * nq, nb - 1), kvcol))],
        out_specs=pl.BlockSpec((nq * BLOCK, ATTN_WIDTH), lambda b, i: (b * ns + i, 0)),
        out_shape=jax.ShapeDtypeStruct((B * S, ATTN_WIDTH), BF16),
        compiler_params=_cparams("parallel", "parallel"), name="window_attn")(sink, proj, proj, proj, proj)


HY_COLS = 256


def _short_conv(u, w, b):
    L = u.shape[0]
    row = lax.broadcasted_iota(jnp.int32, u.shape, 0)
    prev = jnp.where(row == 0, 0.0, pltpu.roll(u, 1, axis=0))
    nxt = jnp.where(row == L - 1, 0.0, pltpu.roll(u, L - 1, axis=0))
    return w[0:1] * prev + w[1:2] * u + w[2:3] * nxt + b


def _hyena_prep_kernel(u0, u1, u2, w0, w1, w2, b0, b1, b2, x0_out, z_out, zb_out):
    x0_out[...] = _short_conv(u0[...], w0[...], b0[...])
    z = _short_conv(u1[...], w1[...], b1[...]) * _short_conv(u2[...], w2[...], b2[...])
    z_out[...] = z
    zb_out[...] = z.astype(BF16)


def _hyena_prep(proj, conv_w, conv_b, B, L):
    nc = HYENA_WIDTH // HY_COLS
    base = HY_OFF // HY_COLS
    u_spec = lambda part: pl.BlockSpec((L, HY_COLS), lambda b, c: (b, base + part * nc + c))
    w_spec = lambda part: pl.BlockSpec((3, HY_COLS), lambda b, c: (0, part * nc + c))
    b_spec = lambda part: pl.BlockSpec((1, HY_COLS), lambda b, c: (0, part * nc + c))
    o_spec = pl.BlockSpec((L, HY_COLS), lambda b, c: (b, c))
    shp = lambda dt: jax.ShapeDtypeStruct((B * L, HYENA_WIDTH), dt)
    return pl.pallas_call(
        _hyena_prep_kernel, grid=(B, nc),
        in_specs=[u_spec(0), u_spec(1), u_spec(2), w_spec(0), w_spec(1), w_spec(2), b_spec(0), b_spec(1), b_spec(2)],
        out_specs=(o_spec, o_spec, o_spec), out_shape=(shp(F32), shp(F32), shp(BF16)),
        compiler_params=_cparams("parallel", "parallel"), name="hyena_prep",
    )(proj, proj, proj, conv_w, conv_w, conv_w, conv_b, conv_b, conv_b)


def _filter_mlp_kernel(z_ref, w1, b1, f1, w2, b2, f2, w3, b3, dec_ref, o_ref):
    h = jnp.sin(f1[...] * (jnp.dot(z_ref[...], w1[...], preferred_element_type=F32) + b1[...]))
    h = jnp.sin(f2[...] * (jnp.dot(h, w2[...], preferred_element_type=F32) + b2[...]))
    h = jnp.dot(h, w3[...], preferred_element_type=F32) + b3[...]
    dec = dec_ref[...]
    o_ref[...] = h * jnp.concatenate([dec, dec], axis=1)


def _filter_mlp(z, w1, b1, f1, w2, b2, f2, w3, b3, decay):
    L = z.shape[0]
    tl = min(512, L)
    fixed = lambda a: pl.BlockSpec(a.shape, lambda i: (0, 0))
    return pl.pallas_call(
        _filter_mlp_kernel, grid=(L // tl,),
        in_specs=[pl.BlockSpec((tl, LANES), lambda i: (i, 0))] + [fixed(a) for a in (w1, b1, f1, w2, b2, f2, w3, b3)]
        + [pl.BlockSpec((tl, HYENA_WIDTH), lambda i: (i, 0))],
        out_specs=pl.BlockSpec((tl, 2 * HYENA_WIDTH), lambda i: (i, 0)),
        out_shape=jax.ShapeDtypeStruct((L, 2 * HYENA_WIDTH), F32),
        compiler_params=_cparams("parallel"), name="filter_mlp")(z, w1, b1, f1, w2, b2, f2, w3, b3, decay)


def _split_bf16(a):
    hi = a.astype(BF16)
    return hi, (a - hi.astype(F32)).astype(BF16)


def _filter_dft_kernel(fre, fim, k0_ref, k1_ref, o_re, o_im, *, n_fft):
    ft = fre.shape[0]
    k = pl.program_id(0) * ft + lax.broadcasted_iota(jnp.int32, (ft, 1), 0)
    sign = (1 - 2 * (k & 1)).astype(F32)
    amp = jnp.where(k == 0, 1.0 / n_fft, 2.0 / n_fft)
    k0h, k0l = _split_bf16(k0_ref[...])
    k1h, k1l = _split_bf16(k1_ref[...])

    def part(f):
        d = lambda a: jnp.dot(f, a, preferred_element_type=F32)
        return amp * ((d(k0h) + d(k0l)) + sign * (d(k1h) + d(k1l)))

    o_re[...] = part(fre[...])
    o_im[...] = part(fim[...])


def _filter_dft(fmat, k0, k1):
    L = k0.shape[0]
    ft = min(FREQ_TILE, L)
    nf = L // ft
    half = pl.BlockSpec((L, HYENA_WIDTH), lambda f: (0, 0))
    o_spec = pl.BlockSpec((ft, HYENA_WIDTH), lambda f: (f, 0))
    shp = jax.ShapeDtypeStruct((L, HYENA_WIDTH), F32)
    return pl.pallas_call(
        functools.partial(_filter_dft_kernel, n_fft=2 * L), grid=(nf,),
        in_specs=[pl.BlockSpec((ft, L), lambda f: (f, 0)), pl.BlockSpec((ft, L), lambda f: (nf + f, 0)), half, half],
        out_specs=(o_spec, o_spec), out_shape=(shp, shp),
        compiler_params=_cparams("parallel"), name="filter_dft")(fmat, fmat, k0, k1)


def _long_conv_kernel(z_ref, fre, fim, gre, gim, kre_ref, kim_ref, y_ref):
    f = pl.program_id(1)
    z = z_ref[...]
    zre = jnp.dot(fre[...], z, preferred_element_type=F32)
    zim = jnp.dot(fim[...], z, preferred_element_type=F32)
    kre, kim = kre_ref[...], kim_ref[...]
    a, b, c, d = zre * kre, zim * kim, zre * kim, zim * kre
    real_row = (lax.broadcasted_iota(jnp.int32, zre.shape, 0) == 0) & (f == 0)
    pre = jnp.where(real_row, a, a - b).astype(BF16)
    pim = jnp.where(real_row, b, c + d).astype(BF16)
    y = jnp.dot(gre[...], pre, preferred_element_type=F32) + jnp.dot(gim[...], pim, preferred_element_type=F32)

    @pl.when(f == 0)
    def _():
        y_ref[...] = y

    @pl.when(f > 0)
    def _():
        y_ref[...] += y


def _long_conv(zb, fmat, gmat, kre, kim, B, L):
    ft = min(FREQ_TILE, L)
    nf = L // ft
    k_spec = pl.BlockSpec((ft, HYENA_WIDTH), lambda b, f: (f, 0))
    return pl.pallas_call(
        _long_conv_kernel, grid=(B, nf),
        in_specs=[pl.BlockSpec((L, HYENA_WIDTH), lambda b, f: (b, 0)),
                  pl.BlockSpec((ft, L), lambda b, f: (f, 0)), pl.BlockSpec((ft, L), lambda b, f: (nf + f, 0)),
                  pl.BlockSpec((L, ft), lambda b, f: (0, f)), pl.BlockSpec((L, ft), lambda b, f: (0, nf + f)),
                  k_spec, k_spec],
        out_specs=pl.BlockSpec((L, HYENA_WIDTH), lambda b, f: (b, 0)),
        out_shape=jax.ShapeDtypeStruct((B * L, HYENA_WIDTH), F32),
        compiler_params=_cparams("parallel", "arbitrary"), name="long_conv")(zb, fmat, fmat, gmat, gmat, kre, kim)


def _dft_matrices(L):
    n = 2 * L

    def build(k, t):
        ang = ((k * t) % n).astype(F32) * (2.0 * math.pi / n)
        re = jnp.cos(ang)
        im = jnp.where(k == 0, (1 - 2 * (t & 1)).astype(F32), -jnp.sin(ang))
        return re.astype(BF16), im.astype(BF16)

    col = jnp.arange(L, dtype=jnp.int32)[:, None]
    row = jnp.arange(L, dtype=jnp.int32)[None, :]
    fmat = jnp.concatenate(build(col, row), axis=0)
    return fmat, fmat.T


def _filter_inputs(L):
    t = jnp.linspace(0.0, 1.0, L, dtype=F32)[:, None]
    bands = (FILTER_EMB_DIM - 1) // 2
    w = 2.0 * math.pi * jnp.arange(L, dtype=F32)[:, None] / L
    f = jnp.linspace(1e-4, bands - 1, bands, dtype=F32)[None, :]
    z = jnp.concatenate([t, jnp.cos(f * w), -jnp.sin(f * w)], axis=-1)
    deltas = jnp.linspace(MIN_DECAY, MAX_DECAY, HYENA_WIDTH, dtype=F32)
    decay = jnp.exp(-t * jnp.abs(deltas)[None, :])
    return jnp.pad(z, ((0, 0), (0, LANES - FILTER_EMB_DIM))), decay


def _pad_to(a, rows, cols):
    return jnp.pad(a, ((0, rows - a.shape[0]), (0, cols - a.shape[1])))


def _hyena_filter_spectrum(L, z_emb, decay, fmat, w1, b1, fr1, w2, b2, fr2, w3, b3):
    h = _filter_mlp(z_emb, _pad_to(w1, LANES, LANES), _pad_to(b1[None], 1, LANES), _pad_to(fr1[None], 1, LANES),
                    _pad_to(w2, LANES, LANES), _pad_to(b2[None], 1, LANES), _pad_to(fr2[None], 1, LANES),
                    _pad_to(w3, LANES, 2 * HYENA_WIDTH), b3[None], decay)
    h_f, h_b = h[:, :HYENA_WIDTH], h[:, HYENA_WIDTH:]
    k0 = jnp.concatenate([h_f[:1] + h_b[:1], h_f[1:]], axis=0)
    k1 = jnp.concatenate([jnp.zeros((1, HYENA_WIDTH), F32), h_b[:0:-1]], axis=0)
    return _filter_dft(fmat, k0, k1)


def _merge_kernel(x_ref, a_ref, x0_ref, z_ref, y_ref, g_ref, skip_ref, wab, whb, wout, nf_ref, wq, h_out, hn_out,
                  q_out):
    ya = jnp.dot(a_ref[...], wab[...], preferred_element_type=F32)
    hy = x0_ref[...] * (y_ref[...] + z_ref[...] * skip_ref[...])
    yh = jnp.dot(hy.astype(BF16), whb[...], preferred_element_type=F32)
    g = g_ref[...]
    merged = jax.nn.sigmoid(g[:, :D_MODEL]) * ya + jax.nn.sigmoid(g[:, D_MODEL:]) * yh
    h = x_ref[...] + jnp.dot(merged.astype(BF16), wout[...], preferred_element_type=F32)
    h_out[...] = h
    hn = _rms(h, nf_ref[...])
    hn_out[...] = hn
    q_out[...] = jnp.dot(hn.astype(BF16), wq[...], preferred_element_type=F32)


def _merge(x, attn, x0, z, y, proj, skip, wab, whb, wout, nf, wq):
    T = x.shape[0]
    tm = min(ROW_TILE, T)
    row = lambda w: pl.BlockSpec((tm, w), lambda i: (i, 0))
    fixed = lambda a: pl.BlockSpec(a.shape, lambda i: (0, 0))
    qw = wq.shape[1]
    return pl.pallas_call(
        _merge_kernel, grid=(T // tm,),
        in_specs=[row(D_MODEL), row(ATTN_WIDTH), row(HYENA_WIDTH), row(HYENA_WIDTH), row(HYENA_WIDTH),
                  row(2 * D_MODEL), fixed(skip), fixed(wab), fixed(whb), fixed(wout), fixed(nf), fixed(wq)],
        out_specs=(row(D_MODEL), row(D_MODEL), row(qw)),
        out_shape=(jax.ShapeDtypeStruct((T, D_MODEL), F32), jax.ShapeDtypeStruct((T, D_MODEL), F32),
                   jax.ShapeDtypeStruct((T, qw), F32)),
        compiler_params=_cparams("parallel"), name="merge_query")(x, attn, x0, z, y, proj, skip, wab, whb, wout, nf, wq)


NEG_INF = float("-inf")
POS_SENTINEL = 1e9


def _top_rows(s, pos, k):
    vals, sel = [], []
    for _ in range(k):
        m = jnp.max(s, axis=0, keepdims=True)
        p = jnp.min(jnp.where(s == m, pos, POS_SENTINEL), axis=0, keepdims=True)
        vals.append(m)
        sel.append(p)
        s = jnp.where(pos == p, NEG_INF, s)
    return vals, sel


def _peer_topk_kernel(q_ref, keys_ref, idx_out, gate_out):
    c = q_ref.shape[0]
    kk = PEER_TOPK
    row_id = lambda n: lax.broadcasted_iota(jnp.int32, (n, c), 0).astype(F32)
    key_pos = row_id(N_KEYS)
    a_small = kk // 2
    b_iota = row_id(a_small)
    idx_rows, gate_rows = [], []
    for h in range(PEER_HEADS):
        tops = []
        for p in range(2):
            qhp = q_ref[:, (2 * h + p) * HALF_QUERY:(2 * h + p + 1) * HALF_QUERY].astype(BF16)
            s = lax.dot_general(keys_ref[p], qhp, (((1,), (1,)), ((), ())), preferred_element_type=F32)
            vals, sel = _top_rows(s, key_pos, kk)
            tops.append((jnp.concatenate(vals, axis=0), jnp.concatenate(sel, axis=0)))
        (s0, i0), (s1, i1) = tops
        cand = [s0[0:1] + s1]
        eidx = [i0[0:1] * N_KEYS + i1]
        cpos = [row_id(kk)]
        for a in range(1, a_small):
            keep = (a + 1) * (b_iota + 1) <= kk
            cand.append(jnp.where(keep, s0[a:a + 1] + s1[:a_small], NEG_INF))
            eidx.append(i0[a:a + 1] * N_KEYS + i1[:a_small])
            cpos.append(a * kk + b_iota)
        cand.append(s0[a_small:] + s1[0:1])
        eidx.append(i0[a_small:] * N_KEYS + i1[0:1])
        cpos.append((b_iota + a_small) * kk)
        cand = jnp.concatenate(cand, axis=0)
        eidx = jnp.concatenate(eidx, axis=0)
        cpos = jnp.concatenate(cpos, axis=0)
        best, sel = _top_rows(cand, cpos, kk)
        chosen = [jnp.max(jnp.where(cpos == p_, eidx, -1.0), axis=0, keepdims=True) for p_ in sel]
        best = jnp.concatenate(best, axis=0)
        e = jnp.exp(best - best[0:1])
        gate_rows.append(e / jnp.sum(e, axis=0, keepdims=True))
        idx_rows.append(jnp.concatenate(chosen, axis=0))
    idx = jnp.concatenate(idx_rows, axis=0)
    gates = jnp.concatenate(gate_rows, axis=0)
    idx_out[...] = (idx * PACK_ROWS).T.astype(jnp.int32)
    gate_out[...] = gates.T


def _peer_topk(q, keys):
    T = q.shape[0]
    c = min(TOPK_TILE, T)
    o_spec = pl.BlockSpec((c, N_SLOTS), lambda i: (i, 0))
    return pl.pallas_call(
        _peer_topk_kernel, grid=(T // c,),
        in_specs=[pl.BlockSpec((c, q.shape[1]), lambda i: (i, 0)), pl.BlockSpec(keys.shape, lambda i: (0, 0, 0))],
        out_specs=(o_spec, o_spec),
        out_shape=(jax.ShapeDtypeStruct((T, N_SLOTS), jnp.int32), jax.ShapeDtypeStruct((T, N_SLOTS), F32)),
        compiler_params=_cparams("parallel"), name="peer_topk")(q, keys)


HI_MASK = 0xFFFF0000
IDX_GROUP = 8


def _unpack(words):
    lo = lax.bitcast_convert_type(words << 16, F32)
    hi = lax.bitcast_convert_type(words & jnp.uint32(HI_MASK), F32)
    return lo, hi


PACK_TILE = 512


def _pack_kernel(t_ref, o_ref):
    t = t_ref[...]
    bf16_bits = lambda a: lax.bitcast_convert_type(a.astype(BF16).astype(F32), jnp.uint32)
    words = (bf16_bits(t[:, :HALF_D]) >> 16) | (bf16_bits(t[:, HALF_D:]) & jnp.uint32(HI_MASK))
    rows = t.shape[0]
    for s in range(PACK_ROWS):
        o_ref[pl.ds(s, rows, stride=PACK_ROWS), :] = words[:, s * LANES:(s + 1) * LANES]


def _pack_table(tables, layer):
    n = tables.shape[1]
    rows = min(PACK_TILE, n)
    return pl.pallas_call(
        _pack_kernel, grid=(n // rows,), in_specs=[pl.BlockSpec((None, rows, D_MODEL), lambda i: (layer, i, 0))],
        out_specs=pl.BlockSpec((rows * PACK_ROWS, LANES), lambda i: (i, 0)),
        out_shape=jax.ShapeDtypeStruct((n * PACK_ROWS, LANES), jnp.uint32),
        compiler_params=_cparams("parallel"), name="pack_table")(tables)


def _gather_rows(idx_ref, tab_ref, tt, stage):
    for g in range(N_SLOTS // IDX_GROUP):
        sub = idx_ref.at[0, 0, pl.ds(tt * N_SLOTS + g * IDX_GROUP, IDX_GROUP)]
        for j in range(IDX_GROUP):
            row0 = pl.multiple_of(sub[j], PACK_ROWS)
            stage[pl.ds(PACK_ROWS * (g * IDX_GROUP + j), PACK_ROWS), :] = tab_ref[pl.ds(row0, PACK_ROWS), :]


def _staged_chunks(stage):
    for s in range(PACK_ROWS):
        lo, hi = _unpack(stage[pl.ds(s, N_SLOTS, stride=PACK_ROWS), :])
        yield s, lo, hi


def _token_pipeline(n_tokens, gather, compute, stages):
    k = len(stages)
    last = n_tokens - 1
    gather(0, stages[0])

    def trip(i, carry):
        t0 = k * i
        for u in range(k):
            nxt = t0 + u + 1
            gather(nxt if u < k - 1 else jnp.minimum(nxt, last), stages[(u + 1) % k])
            compute(t0 + u, stages[u])
        return carry

    lax.fori_loop(0, n_tokens // k, trip, 0)


def _expert_act_kernel(idx_ref, hn_ref, gate_ref, tab_ref, w_out, *stages):
    def compute(tt, stage):
        x = hn_ref[tt]
        acc = jnp.zeros((N_SLOTS, LANES), F32)
        for s, lo, hi in _staged_chunks(stage):
            acc = acc + lo * x[s:s + 1] + hi * x[PACK_ROWS + s:PACK_ROWS + s + 1]
        act = jnp.sum(acc.T, axis=0, keepdims=True)
        gelu = 0.5 * act * (1.0 + lax.erf(act * (2.0 ** -0.5)))
        w_out[tt] = gate_ref[tt] * gelu

    _token_pipeline(hn_ref.shape[0], functools.partial(_gather_rows, idx_ref, tab_ref), compute, stages)


N_STAGES = 4


def _stage_scratch():
    return pltpu.VMEM((N_SLOTS * PACK_ROWS, LANES), jnp.uint32)


def _expert_act(idx3, hn3, gate3, table):
    T = hn3.shape[0]
    tb = idx3.shape[2] // N_SLOTS
    return pl.pallas_call(
        _expert_act_kernel, grid=(T // tb,),
        in_specs=[pl.BlockSpec((1, 1, tb * N_SLOTS), lambda i: (i, 0, 0), memory_space=pltpu.SMEM),
                  pl.BlockSpec((tb, SUBLANES, LANES), lambda i: (i, 0, 0)),
                  pl.BlockSpec((tb, 1, N_SLOTS), lambda i: (i, 0, 0)),
                  pl.BlockSpec(memory_space=pltpu.VMEM)],
        out_specs=pl.BlockSpec((tb, 1, N_SLOTS), lambda i: (i, 0, 0)),
        out_shape=jax.ShapeDtypeStruct((T, 1, N_SLOTS), F32),
        scratch_shapes=[_stage_scratch()] * N_STAGES,
        compiler_params=_cparams("arbitrary"), name="expert_act")(idx3, hn3, gate3, table)


def _expert_sum_kernel(idx_ref, w_ref, tab_ref, o_ref, *stages):
    def compute(tt, stage):
        w_col = jnp.broadcast_to(w_ref[tt], (LANES, N_SLOTS)).T
        lo_rows, hi_rows = [], []
        for _, lo, hi in _staged_chunks(stage):
            lo_rows.append(jnp.sum(w_col * lo, axis=0, keepdims=True))
            hi_rows.append(jnp.sum(w_col * hi, axis=0, keepdims=True))
        o_ref[tt] = jnp.concatenate(lo_rows + hi_rows, axis=0)

    _token_pipeline(o_ref.shape[0], functools.partial(_gather_rows, idx_ref, tab_ref), compute, stages)


def _expert_sum(idx3, w3, table):
    T = w3.shape[0]
    tb = idx3.shape[2] // N_SLOTS
    return pl.pallas_call(
        _expert_sum_kernel, grid=(T // tb,),
        in_specs=[pl.BlockSpec((1, 1, tb * N_SLOTS), lambda i: (i, 0, 0), memory_space=pltpu.SMEM),
                  pl.BlockSpec((tb, 1, N_SLOTS), lambda i: (i, 0, 0)),
                  pl.BlockSpec(memory_space=pltpu.VMEM)],
        out_specs=pl.BlockSpec((tb, SUBLANES, LANES), lambda i: (i, 0, 0)),
        out_shape=jax.ShapeDtypeStruct((T, SUBLANES, LANES), F32),
        scratch_shapes=[_stage_scratch()] * N_STAGES,
        compiler_params=_cparams("arbitrary"), name="expert_sum")(idx3, w3, table)


def _final_kernel(h_ref, p_ref, g_ref, o_ref):
    o_ref[...] = _rms(h_ref[...] + _load_rows3(p_ref), g_ref[...])


def _final_norm(h, p, g):
    T = h.shape[0]
    tm = min(ROW_TILE, T)
    row = pl.BlockSpec((tm, D_MODEL), lambda i: (i, 0))
    return pl.pallas_call(
        _final_kernel, grid=(T // tm,),
        in_specs=[row, pl.BlockSpec((tm, SUBLANES, LANES), lambda i: (i, 0, 0)), pl.BlockSpec((1, D_MODEL), lambda i: (0, 0))],
        out_specs=row, out_shape=jax.ShapeDtypeStruct((T, D_MODEL), F32),
        compiler_params=_cparams("parallel"), name="final_norm")(h, p, g)


def _rope_tables(positions):
    half = ROPE_DIM // 2
    inv_freq = ROPE_THETA ** (-jnp.arange(0, ROPE_DIM, 2, dtype=F32) / ROPE_DIM)
    ang = positions.reshape(-1).astype(F32)[:, None] * inv_freq
    cos, sin = jnp.cos(ang), jnp.sin(ang)
    T = cos.shape[0]
    pad = HEAD_DIM - ROPE_DIM
    head = lambda first, second, fill: jnp.concatenate(
        [first, second, jnp.full((T, pad), fill, F32)], axis=1)
    zeros = jnp.zeros_like(sin)
    reps = LANES // HEAD_DIM
    return tuple(jnp.tile(head(*parts), (1, reps))
                 for parts in ((cos, cos, 1.0), (-sin, zeros, 0.0), (zeros, sin, 0.0)))


def _reorder_in_proj(w):
    a0 = ATTN_WIDTH
    a1, a2 = a0 + KV_WIDTH, a0 + 2 * KV_WIDTH
    a3 = a2 + 3 * HYENA_WIDTH
    q, k, v, hy, gates = w[:, :a0], w[:, a0:a1], w[:, a1:a2], w[:, a2:a3], w[:, a3:]
    return jnp.concatenate([gates, hy, q, k, v], axis=1).astype(BF16)


def kernel(x, positions, norm_mix, w_in, conv_w, conv_b, filt_w1, filt_b1, filt_freq1, filt_w2, filt_b2, filt_freq2,
           filt_w3, filt_b3, hyena_skip, attn_sink, w_attn_branch, w_hyena_branch, w_out, norm_ffn, w_query,
           sub_keys, expert_u, expert_v, norm_final):
    B, S, D = x.shape
    assert D == D_MODEL and S % BLOCK == 0
    T = B * S
    depth = w_in.shape[0]
    tb = min(GATHER_TOKENS, T)
    cos_t, s1_t, s2_t = _rope_tables(positions)
    fmat, gmat = _dft_matrices(S)
    z_emb, decay = _filter_inputs(S)

    h = x.reshape(T, D)
    peer = None
    for l in range(depth):
        res = _in_proj(h, peer, norm_mix[l][None], _reorder_in_proj(w_in[l]), cos_t, s1_t, s2_t)
        xcur, proj = (h, res) if peer is None else res
        attn = _attention(proj, attn_sink[l], B, S)
        x0, z, zb = _hyena_prep(proj, conv_w[l], conv_b[l][None], B, S)
        kre, kim = _hyena_filter_spectrum(S, z_emb, decay, fmat, filt_w1[l], filt_b1[l], filt_freq1[l], filt_w2[l],
                                          filt_b2[l], filt_freq2[l], filt_w3[l], filt_b3[l])
        ylong = _long_conv(zb, fmat, gmat, kre, kim, B, S)
        h, hn, q = _merge(xcur, attn, x0, z, ylong, proj, hyena_skip[l][None], w_attn_branch[l].astype(BF16),
                          w_hyena_branch[l].astype(BF16), w_out[l].astype(BF16), norm_ffn[l][None],
                          w_query[l].astype(BF16))
        idx, gates = _peer_topk(q, sub_keys[l].astype(BF16))
        idx3 = idx.reshape(T // tb, 1, tb * N_SLOTS)
        wts = _expert_act(idx3, hn.reshape(T, SUBLANES, LANES), gates.reshape(T, 1, N_SLOTS),
                          _pack_table(expert_u, l))
        peer = _expert_sum(idx3, wts, _pack_table(expert_v, l))
    return _final_norm(h, peer, norm_final[None]).reshape(B, S, D)
```

```python
import functools
import math

import jax
import jax.numpy as jnp
from jax import lax
from jax.experimental import pallas as pl
from jax.experimental.pallas import tpu as pltpu

F32 = jnp.float32
BF16 = jnp.bfloat16

D_MODEL = 1024
N_Q_HEADS = 8
N_KV_HEADS = 2
HEAD_DIM = 64
GQA_GROUP = N_Q_HEADS // N_KV_HEADS
ATTN_WIDTH = N_Q_HEADS * HEAD_DIM
KV_WIDTH = N_KV_HEADS * HEAD_DIM
WINDOW = 128
BLOCK = 128
ROPE_DIM = HEAD_DIM // 4
ROPE_THETA = 500000.0
HYENA_WIDTH = 512
FILTER_EMB_DIM = 33
FILTER_HIDDEN = 64
DECAY_TARGET = 1e-2
MAX_DECAY = math.log(DECAY_TARGET) / 0.3
MIN_DECAY = math.log(DECAY_TARGET) / 1.5
N_KEYS = 128
N_EXPERTS = N_KEYS * N_KEYS
PEER_HEADS = 8
PEER_TOPK = 16
HALF_QUERY = 128
N_SLOTS = PEER_HEADS * PEER_TOPK
EPS = 1e-6

LANES = 128
SUBLANES = 8
VMEM_LIMIT_BYTES = 56 * 1024 * 1024

GATE_OFF = 0
HY_OFF = 2 * D_MODEL
Q_OFF = HY_OFF + 3 * HYENA_WIDTH
K_OFF = Q_OFF + ATTN_WIDTH
V_OFF = K_OFF + KV_WIDTH
IN_WIDTH = V_OFF + KV_WIDTH
ROPE_WIDTH = ATTN_WIDTH + KV_WIDTH

ROW_TILE = 256
FREQ_TILE = 512
TOPK_TILE = 256
GATHER_TOKENS = 64
HALF_D = D_MODEL // 2
PACK_ROWS = HALF_D // LANES


def _cparams(*sem):
    return pltpu.CompilerParams(dimension_semantics=sem, vmem_limit_bytes=VMEM_LIMIT_BYTES)


def _load_rows3(ref):
    return jnp.concatenate([ref[:, s, :] for s in range(SUBLANES)], axis=1)


def _rms(x, g):
    r = lax.rsqrt(jnp.mean(x * x, axis=-1, keepdims=True) + EPS)
    return x * r * g


def _in_proj_kernel(*refs, has_add):
    if has_add:
        h_ref, p_ref, g_ref, w_ref, c_ref, s1_ref, s2_ref, x_out, proj_out = refs
        x = h_ref[...] + _load_rows3(p_ref)
        x_out[...] = x
    else:
        h_ref, g_ref, w_ref, c_ref, s1_ref, s2_ref, proj_out = refs
        x = h_ref[...]
    xn = _rms(x, g_ref[...]).astype(BF16)
    proj_out[:, :Q_OFF] = jnp.dot(xn, w_ref[:, :Q_OFF], preferred_element_type=F32)
    qk = jnp.dot(xn, w_ref[:, Q_OFF:V_OFF], preferred_element_type=F32)
    reps = ROPE_WIDTH // LANES
    cos = jnp.concatenate([c_ref[...]] * reps, axis=1)
    s1 = jnp.concatenate([s1_ref[...]] * reps, axis=1)
    s2 = jnp.concatenate([s2_ref[...]] * reps, axis=1)
    half = ROPE_DIM // 2
    rot = qk * cos + pltpu.roll(qk, ROPE_WIDTH - half, axis=1) * s1 + pltpu.roll(qk, half, axis=1) * s2
    proj_out[:, Q_OFF:V_OFF] = rot
    proj_out[:, V_OFF:] = jnp.dot(xn, w_ref[:, V_OFF:], preferred_element_type=F32)


def _in_proj(h, p, g, w, cos_t, s1_t, s2_t):
    T = h.shape[0]
    tm = min(ROW_TILE, T)
    has_add = p is not None
    row = lambda i: (i, 0)
    fixed = lambda i: (0, 0)
    x_spec = pl.BlockSpec((tm, D_MODEL), row)
    tab_spec = pl.BlockSpec((tm, LANES), row)
    in_specs = [x_spec] + ([pl.BlockSpec((tm, SUBLANES, LANES), lambda i: (i, 0, 0))] if has_add else []) + [
        pl.BlockSpec((1, D_MODEL), fixed), pl.BlockSpec((D_MODEL, IN_WIDTH), fixed), tab_spec, tab_spec, tab_spec]
    proj_shape = jax.ShapeDtypeStruct((T, IN_WIDTH), F32)
    proj_spec = pl.BlockSpec((tm, IN_WIDTH), row)
    if has_add:
        out_shape, out_specs = (jax.ShapeDtypeStruct((T, D_MODEL), F32), proj_shape), (x_spec, proj_spec)
        args = (h, p, g, w, cos_t, s1_t, s2_t)
    else:
        out_shape, out_specs = proj_shape, proj_spec
        args = (h, g, w, cos_t, s1_t, s2_t)
    return pl.pallas_call(
        functools.partial(_in_proj_kernel, has_add=has_add), grid=(T // tm,), in_specs=in_specs,
        out_specs=out_specs, out_shape=out_shape, compiler_params=_cparams("parallel"), name="in_proj")(*args)


ATTN_Q_BLOCKS = 2


def _attn_kernel(sink_ref, q_ref, kp_ref, ko_ref, kn_ref, o_ref, *, n_steps):
    i = pl.program_id(1)
    kv = jnp.concatenate([kp_ref[...], ko_ref[...], kn_ref[...]], axis=0)
    c = lax.broadcasted_iota(jnp.int32, (3 * BLOCK, BLOCK), 0)
    r = lax.broadcasted_iota(jnp.int32, (3 * BLOCK, BLOCK), 1)
    d = c - r
    band = (d >= BLOCK - WINDOW) & (d <= BLOCK + WINDOW)
    scale = HEAD_DIM ** -0.5
    for u in range(ATTN_Q_BLOCKS):
        valid = band
        if u == 0:
            valid = valid & ((c >= BLOCK) | (i > 0))
        if u == ATTN_Q_BLOCKS - 1:
            valid = valid & ((c < 2 * BLOCK) | (i < n_steps - 1))
        valid = jnp.concatenate([valid] * GQA_GROUP, axis=1)
        q = q_ref[u * BLOCK:(u + 1) * BLOCK, :] * scale
        kv_u = kv[u * BLOCK:(u + 3) * BLOCK]
        v_t = kv_u[:, KV_WIDTH:].T.astype(BF16)
        for g in range(N_KV_HEADS):
            k = kv_u[:, g * HEAD_DIM:(g + 1) * HEAD_DIM].astype(BF16)
            heads = [g * GQA_GROUP + hh for hh in range(GQA_GROUP)]
            qs = jnp.concatenate([q[:, h * HEAD_DIM:(h + 1) * HEAD_DIM] for h in heads], axis=0).astype(BF16)
            s = lax.dot_general(k, qs, (((1,), (1,)), ((), ())), preferred_element_type=F32)
            s = jnp.where(valid, s, -1e30)
            sink = jnp.concatenate([jnp.full((1, BLOCK), sink_ref[h], F32) for h in heads], axis=1)
            m = jnp.maximum(jnp.max(s, axis=0, keepdims=True), sink)
            p = jnp.exp(s - m)
            denom = jnp.sum(p, axis=0, keepdims=True) + jnp.exp(sink - m)
            o_t = jnp.dot(v_t[g * HEAD_DIM:(g + 1) * HEAD_DIM], p.astype(BF16), preferred_element_type=F32)
            o_t = o_t * (1.0 / denom)
            for hh in range(0, GQA_GROUP, 2):
                pair = jnp.concatenate([o_t[:, hh * BLOCK:(hh + 1) * BLOCK],
                                        o_t[:, (hh + 1) * BLOCK:(hh + 2) * BLOCK]], axis=0).T
                col = heads[hh] * HEAD_DIM
                o_ref[u * BLOCK:(u + 1) * BLOCK, col:col + 2 * HEAD_DIM] = pair.astype(o_ref.dtype)


def _attention(proj, sink, B, S):
    nb = S // BLOCK
    nq = ATTN_Q_BLOCKS
    assert nb % nq == 0
    ns = nb // nq
    qcol = Q_OFF // ATTN_WIDTH
    kvcol = K_OFF // (2 * KV_WIDTH)
    edge_spec = lambda f: pl.BlockSpec((BLOCK, 2 * KV_WIDTH), f)
    return pl.pallas_call(
        functools.partial(_attn_kernel, n_steps=ns), grid=(B, ns),
        in_specs=[pl.BlockSpec(memory_space=pltpu.SMEM),
                  pl.BlockSpec((nq * BLOCK, ATTN_WIDTH), lambda b, i: (b * ns + i, qcol)),
                  edge_spec(lambda b, i: (b * nb + jnp.maximum(i * nq - 1, 0), kvcol)),
                  pl.BlockSpec((nq * BLOCK, 2 * KV_WIDTH), lambda b, i: (b * ns + i, kvcol)),
                  edge_spec(lambda b, i: (b * nb + jnp.minimum((i + 1) * nq, nb - 1), kvcol))],
        out_specs=pl.BlockSpec((nq * BLOCK, ATTN_WIDTH), lambda b, i: (b * ns + i, 0)),
        out_shape=jax.ShapeDtypeStruct((B * S, ATTN_WIDTH), BF16),
        compiler_params=_cparams("parallel", "parallel"), name="window_attn")(sink, proj, proj, proj, proj)


HY_COLS = 256


def _short_conv(u, w, b):
    L = u.shape[0]
    row = lax.broadcasted_iota(jnp.int32, u.shape, 0)
    prev = jnp.where(row == 0, 0.0, pltpu.roll(u, 1, axis=0))
    nxt = jnp.where(row == L - 1, 0.0, pltpu.roll(u, L - 1, axis=0))
    return w[0:1] * prev + w[1:2] * u + w[2:3] * nxt + b


def _hyena_prep_kernel(u0, u1, u2, w0, w1, w2, b0, b1, b2, x0_out, z_out, zb_out):
    x0_out[...] = _short_conv(u0[...], w0[...], b0[...])
    z = _short_conv(u1[...], w1[...], b1[...]) * _short_conv(u2[...], w2[...], b2[...])
    z_out[...] = z
    zb_out[...] = z.astype(BF16)


def _hyena_prep(proj, conv_w, conv_b, B, L):
    nc = HYENA_WIDTH // HY_COLS
    base = HY_OFF // HY_COLS
    u_spec = lambda part: pl.BlockSpec((L, HY_COLS), lambda b, c: (b, base + part * nc + c))
    w_spec = lambda part: pl.BlockSpec((3, HY_COLS), lambda b, c: (0, part * nc + c))
    b_spec = lambda part: pl.BlockSpec((1, HY_COLS), lambda b, c: (0, part * nc + c))
    o_spec = pl.BlockSpec((L, HY_COLS), lambda b, c: (b, c))
    shp = lambda dt: jax.ShapeDtypeStruct((B * L, HYENA_WIDTH), dt)
    return pl.pallas_call(
        _hyena_prep_kernel, grid=(B, nc),
        in_specs=[u_spec(0), u_spec(1), u_spec(2), w_spec(0), w_spec(1), w_spec(2), b_spec(0), b_spec(1), b_spec(2)],
        out_specs=(o_spec, o_spec, o_spec), out_shape=(shp(F32), shp(F32), shp(BF16)),
        compiler_params=_cparams("parallel", "parallel"), name="hyena_prep",
    )(proj, proj, proj, conv_w, conv_w, conv_w, conv_b, conv_b, conv_b)


def _filter_mlp_kernel(z_ref, w1, b1, f1, w2, b2, f2, w3, b3, dec_ref, o_ref):
    h = jnp.sin(f1[...] * (jnp.dot(z_ref[...], w1[...], preferred_element_type=F32) + b1[...]))
    h = jnp.sin(f2[...] * (jnp.dot(h, w2[...], preferred_element_type=F32) + b2[...]))
    h = jnp.dot(h, w3[...], preferred_element_type=F32) + b3[...]
    dec = dec_ref[...]
    o_ref[...] = h * jnp.concatenate([dec, dec], axis=1)


def _filter_mlp(z, w1, b1, f1, w2, b2, f2, w3, b3, decay):
    L = z.shape[0]
    tl = min(FREQ_TILE, L)
    fixed = lambda a: pl.BlockSpec(a.shape, lambda i: (0, 0))
    return pl.pallas_call(
        _filter_mlp_kernel, grid=(L // tl,),
        in_specs=[pl.BlockSpec((tl, LANES), lambda i: (i, 0))] + [fixed(a) for a in (w1, b1, f1, w2, b2, f2, w3, b3)]
        + [pl.BlockSpec((tl, HYENA_WIDTH), lambda i: (i, 0))],
        out_specs=pl.BlockSpec((tl, 2 * HYENA_WIDTH), lambda i: (i, 0)),
        out_shape=jax.ShapeDtypeStruct((L, 2 * HYENA_WIDTH), F32),
        compiler_params=_cparams("parallel"), name="filter_mlp")(z, w1, b1, f1, w2, b2, f2, w3, b3, decay)


def _split_bf16(a):
    hi = a.astype(BF16)
    return hi, (a - hi.astype(F32)).astype(BF16)


def _filter_dft_kernel(fre, fim, k0_ref, k1_ref, o_re, o_im, *, n_fft):
    ft = fre.shape[0]
    k = pl.program_id(0) * ft + lax.broadcasted_iota(jnp.int32, (ft, 1), 0)
    sign_im = jnp.where(k == 0, 1.0, -1.0)
    amp = jnp.where(k == 0, 1.0 / n_fft, 2.0 / n_fft)
    k0h, k0l = _split_bf16(k0_ref[...])
    k1h, k1l = _split_bf16(k1_ref[...])

    def part(f, sign):
        d = lambda a: jnp.dot(f, a, preferred_element_type=F32)
        return amp * ((d(k0h) + d(k0l)) + sign * (d(k1h) + d(k1l)))

    o_re[...] = part(fre[...], 1.0)
    o_im[...] = part(fim[...], sign_im)


def _filter_dft(fmat, k0, k1):
    L = k0.shape[0]
    ft = min(FREQ_TILE, L)
    nf = L // ft
    half = pl.BlockSpec((L, HYENA_WIDTH), lambda f: (0, 0))
    o_spec = pl.BlockSpec((ft, HYENA_WIDTH), lambda f: (f, 0))
    shp = jax.ShapeDtypeStruct((L, HYENA_WIDTH), F32)
    return pl.pallas_call(
        functools.partial(_filter_dft_kernel, n_fft=2 * L), grid=(nf,),
        in_specs=[pl.BlockSpec((ft, L), lambda f: (f, 0)), pl.BlockSpec((ft, L), lambda f: (nf + f, 0)), half, half],
        out_specs=(o_spec, o_spec), out_shape=(shp, shp),
        compiler_params=_cparams("parallel"), name="filter_dft")(fmat, fmat, k0, k1)


def _long_conv_kernel(z_ref, fre, fim, gre, gim, kre_ref, kim_ref, y_ref):
    f = pl.program_id(1)
    z = z_ref[...]
    zre = jnp.dot(fre[...], z, preferred_element_type=F32)
    zim = jnp.dot(fim[...], z, preferred_element_type=F32)
    kre, kim = kre_ref[...], kim_ref[...]
    a, b, c, d = zre * kre, zim * kim, zre * kim, zim * kre
    real_row = (lax.broadcasted_iota(jnp.int32, zre.shape, 0) == 0) & (f == 0)
    pre = jnp.where(real_row, a, a - b).astype(BF16)
    pim = jnp.where(real_row, b, c + d).astype(BF16)
    y = jnp.dot(gre[...], pre, preferred_element_type=F32) + jnp.dot(gim[...], pim, preferred_element_type=F32)

    @pl.when(f == 0)
    def _():
        y_ref[...] = y

    @pl.when(f > 0)
    def _():
        y_ref[...] += y


def _long_conv(zb, fmat, gmat, kre, kim, B, L):
    ft = min(FREQ_TILE, L)
    nf = L // ft
    k_spec = pl.BlockSpec((ft, HYENA_WIDTH), lambda b, f: (f, 0))
    return pl.pallas_call(
        _long_conv_kernel, grid=(B, nf),
        in_specs=[pl.BlockSpec((L, HYENA_WIDTH), lambda b, f: (b, 0)),
                  pl.BlockSpec((ft, L), lambda b, f: (f, 0)), pl.BlockSpec((ft, L), lambda b, f: (nf + f, 0)),
                  pl.BlockSpec((L, ft), lambda b, f: (0, f)), pl.BlockSpec((L, ft), lambda b, f: (0, nf + f)),
                  k_spec, k_spec],
        out_specs=pl.BlockSpec((L, HYENA_WIDTH), lambda b, f: (b, 0)),
        out_shape=jax.ShapeDtypeStruct((B * L, HYENA_WIDTH), F32),
        compiler_params=_cparams("parallel", "arbitrary"), name="long_conv")(zb, fmat, fmat, gmat, gmat, kre, kim)


def _dft_matrices(L):
    n = 2 * L

    def build(k, t):
        ang = ((k * t) % n).astype(F32) * (2.0 * math.pi / n)
        re = jnp.cos(ang)
        im = jnp.where(k == 0, (1 - 2 * (t & 1)).astype(F32), -jnp.sin(ang))
        return re.astype(BF16), im.astype(BF16)

    col = jnp.arange(L, dtype=jnp.int32)[:, None]
    row = jnp.arange(L, dtype=jnp.int32)[None, :]
    fmat = jnp.concatenate(build(col, row), axis=0)
    return fmat, fmat.T


def _filter_inputs(L):
    t = jnp.linspace(0.0, 1.0, L, dtype=F32)[:, None]
    bands = (FILTER_EMB_DIM - 1) // 2
    w = 2.0 * math.pi * jnp.arange(L, dtype=F32)[:, None] / L
    f = jnp.linspace(1e-4, bands - 1, bands, dtype=F32)[None, :]
    z = jnp.concatenate([t, jnp.cos(f * w), -jnp.sin(f * w)], axis=-1)
    deltas = jnp.linspace(MIN_DECAY, MAX_DECAY, HYENA_WIDTH, dtype=F32)
    decay = jnp.exp(-t * jnp.abs(deltas)[None, :])
    return jnp.pad(z, ((0, 0), (0, LANES - FILTER_EMB_DIM))), decay


def _pad_to(a, rows, cols):
    return jnp.pad(a, ((0, rows - a.shape[0]), (0, cols - a.shape[1])))


def _hyena_filter_spectrum(L, z_emb, decay, fmat, w1, b1, fr1, w2, b2, fr2, w3, b3):
    h = _filter_mlp(z_emb, _pad_to(w1, LANES, LANES), _pad_to(b1[None], 1, LANES), _pad_to(fr1[None], 1, LANES),
                    _pad_to(w2, LANES, LANES), _pad_to(b2[None], 1, LANES), _pad_to(fr2[None], 1, LANES),
                    _pad_to(w3, LANES, 2 * HYENA_WIDTH), b3[None], decay)
    h_f, h_b = h[:, :HYENA_WIDTH], h[:, HYENA_WIDTH:]
    k0 = jnp.concatenate([h_f[:1] + h_b[:1], h_f[1:]], axis=0)
    k1 = jnp.concatenate([jnp.zeros((1, HYENA_WIDTH), F32), h_b[1:]], axis=0)
    return _filter_dft(fmat, k0, k1)


def _merge_kernel(x_ref, a_ref, x0_ref, z_ref, y_ref, g_ref, skip_ref, wab, whb, wout, nf_ref, wq, h_out, hn_out,
                  q_out):
    ya = jnp.dot(a_ref[...], wab[...], preferred_element_type=F32)
    hy = x0_ref[...] * (y_ref[...] + z_ref[...] * skip_ref[...])
    yh = jnp.dot(hy.astype(BF16), whb[...], preferred_element_type=F32)
    g = g_ref[...]
    merged = jax.nn.sigmoid(g[:, :D_MODEL]) * ya + jax.nn.sigmoid(g[:, D_MODEL:]) * yh
    h = x_ref[...] + jnp.dot(merged.astype(BF16), wout[...], preferred_element_type=F32)
    h_out[...] = h
    hn = _rms(h, nf_ref[...])
    hn_out[...] = hn
    q_out[...] = jnp.dot(hn.astype(BF16), wq[...], preferred_element_type=F32)


def _merge(x, attn, x0, z, y, proj, skip, wab, whb, wout, nf, wq):
    T = x.shape[0]
    tm = min(ROW_TILE, T)
    row = lambda w: pl.BlockSpec((tm, w), lambda i: (i, 0))
    fixed = lambda a: pl.BlockSpec(a.shape, lambda i: (0, 0))
    qw = wq.shape[1]
    return pl.pallas_call(
        _merge_kernel, grid=(T // tm,),
        in_specs=[row(D_MODEL), row(ATTN_WIDTH), row(HYENA_WIDTH), row(HYENA_WIDTH), row(HYENA_WIDTH),
                  row(2 * D_MODEL), fixed(skip), fixed(wab), fixed(whb), fixed(wout), fixed(nf), fixed(wq)],
        out_specs=(row(D_MODEL), row(D_MODEL), row(qw)),
        out_shape=(jax.ShapeDtypeStruct((T, D_MODEL), F32), jax.ShapeDtypeStruct((T, D_MODEL), F32),
                   jax.ShapeDtypeStruct((T, qw), F32)),
        compiler_params=_cparams("parallel"), name="merge_query")(x, attn, x0, z, y, proj, skip, wab, whb, wout, nf, wq)


NEG_INF = float("-inf")
POS_SENTINEL = 1e9


def _top_rows(s, pos, k, payload=None):
    vals, sel, picked = [], [], []
    for _ in range(k):
        m = jnp.max(s, axis=0, keepdims=True)
        p = jnp.min(jnp.where(s == m, pos, POS_SENTINEL), axis=0, keepdims=True)
        hit = pos == p
        vals.append(m)
        sel.append(p)
        if payload is not None:
            picked.append(jnp.max(jnp.where(hit, payload, -1.0), axis=0, keepdims=True))
        s = jnp.where(hit, NEG_INF, s)
    return vals, sel, picked


def _top_rows_paired(s, pos_a, k):
    half = s.shape[0] // 2
    a, b = s[:half], s[half:]
    pos_b = pos_a + float(half)
    b_wins = b > a
    front, back = jnp.maximum(a, b), jnp.minimum(a, b)
    front_pos = jnp.where(b_wins, pos_b, pos_a)
    back_pos = jnp.where(b_wins, pos_a, pos_b)
    vals, sel = [], []
    for _ in range(k):
        m = jnp.max(front, axis=0, keepdims=True)
        p = jnp.min(jnp.where(front == m, front_pos, POS_SENTINEL), axis=0, keepdims=True)
        hit = front_pos == p
        vals.append(m)
        sel.append(p)
        front = jnp.where(hit, back, front)
        front_pos = jnp.where(hit, back_pos, front_pos)
        back = jnp.where(hit, NEG_INF, back)
    return vals, sel


def _candidate_segments(kk):
    runs = [(a, kk // (a + 1)) for a in range(kk) if kk // (a + 1) > 1]
    pieces = []
    for a, n in runs:
        pieces += [("row", a, b0, min(b0 + SUBLANES, n)) for b0 in range(0, n, SUBLANES)]
    groups, free = [], []
    for piece in sorted(pieces, key=lambda t: t[2] - t[3]):
        n = piece[3] - piece[2]
        for g, room in enumerate(free):
            if room >= n:
                groups[g].append(piece)
                free[g] -= n
                break
        else:
            groups.append([piece])
            free.append(SUBLANES - n)
    a0 = len(runs)
    for g, room in enumerate(free):
        take = min(room, kk - a0)
        if take:
            groups[g].append(("col", a0, a0 + take))
            a0 += take
    while a0 < kk:
        take = min(SUBLANES, kk - a0)
        groups.append([("col", a0, a0 + take)])
        a0 += take
    return [piece for group in groups for piece in group]


def _peer_topk_kernel(q_ref, keys_ref, idx_out, gate_out):
    c = q_ref.shape[0]
    kk = PEER_TOPK
    row_id = lambda n: lax.broadcasted_iota(jnp.int32, (n, c), 0).astype(F32)
    half_pos = row_id(N_KEYS // 2)
    segments = _candidate_segments(kk)
    n_cand = sum(t[-1] - t[-2] for t in segments)
    pad = -n_cand % SUBLANES
    cpos = [(t[1] * kk + t[2]) + row_id(t[3] - t[2]) if t[0] == "row" else (t[1] + row_id(t[2] - t[1])) * kk
            for t in segments]
    cpos = jnp.concatenate(cpos + [jnp.full((pad, c), POS_SENTINEL, F32)] * (pad > 0), axis=0)
    idx_rows, gate_rows = [], []
    for h in range(PEER_HEADS):
        tops = []
        for p in range(2):
            qhp = q_ref[:, (2 * h + p) * HALF_QUERY:(2 * h + p + 1) * HALF_QUERY].astype(BF16)
            s = lax.dot_general(keys_ref[p], qhp, (((1,), (1,)), ((), ())), preferred_element_type=F32)
            vals, sel = _top_rows_paired(s, half_pos, kk)
            tops.append((jnp.concatenate(vals, axis=0), jnp.concatenate(sel, axis=0)))
        (s0, i0), (s1, i1) = tops
        cand, eidx = [], []
        for t in segments:
            if t[0] == "row":
                _, a, b0, b1 = t
                cand.append(s0[a:a + 1] + s1[b0:b1])
                eidx.append(i0[a:a + 1] * N_KEYS + i1[b0:b1])
            else:
                _, a0, a1 = t
                cand.append(s0[a0:a1] + s1[0:1])
                eidx.append(i0[a0:a1] * N_KEYS + i1[0:1])
        if pad:
            cand.append(jnp.full((pad, c), NEG_INF, F32))
            eidx.append(jnp.full((pad, c), -1.0, F32))
        best, _, chosen = _top_rows(jnp.concatenate(cand, axis=0), cpos, kk, payload=jnp.concatenate(eidx, axis=0))
        best = jnp.concatenate(best, axis=0)
        e = jnp.exp(best - best[0:1])
        gate_rows.append(e / jnp.sum(e, axis=0, keepdims=True))
        idx_rows.append(jnp.concatenate(chosen, axis=0))
    idx = jnp.concatenate(idx_rows, axis=0)
    gates = jnp.concatenate(gate_rows, axis=0)
    idx_out[...] = (idx * PACK_ROWS).T.astype(jnp.int32)
    gate_out[...] = gates.T


def _peer_topk(q, keys):
    T = q.shape[0]
    c = min(TOPK_TILE, T)
    o_spec = pl.BlockSpec((c, N_SLOTS), lambda i: (i, 0))
    return pl.pallas_call(
        _peer_topk_kernel, grid=(T // c,),
        in_specs=[pl.BlockSpec((c, q.shape[1]), lambda i: (i, 0)), pl.BlockSpec(keys.shape, lambda i: (0, 0, 0))],
        out_specs=(o_spec, o_spec),
        out_shape=(jax.ShapeDtypeStruct((T, N_SLOTS), jnp.int32), jax.ShapeDtypeStruct((T, N_SLOTS), F32)),
        compiler_params=_cparams("parallel"), name="peer_topk")(q, keys)


HI_MASK = 0xFFFF0000
IDX_GROUP = 8


def _unpack(words):
    lo = lax.bitcast_convert_type(words << 16, F32)
    hi = lax.bitcast_convert_type(words & jnp.uint32(HI_MASK), F32)
    return lo, hi


PACK_TILE = 512


def _pack_kernel(t_ref, o_ref):
    t = t_ref[...]
    bf16_bits = lambda a: lax.bitcast_convert_type(a.astype(BF16).astype(F32), jnp.uint32)
    words = (bf16_bits(t[:, :HALF_D]) >> 16) | (bf16_bits(t[:, HALF_D:]) & jnp.uint32(HI_MASK))
    rows = t.shape[0]
    for s in range(PACK_ROWS):
        o_ref[pl.ds(s, rows, stride=PACK_ROWS), :] = words[:, s * LANES:(s + 1) * LANES]


def _pack_table(tables, layer):
    n = tables.shape[1]
    rows = min(PACK_TILE, n)
    return pl.pallas_call(
        _pack_kernel, grid=(n // rows,), in_specs=[pl.BlockSpec((None, rows, D_MODEL), lambda i: (layer, i, 0))],
        out_specs=pl.BlockSpec((rows * PACK_ROWS, LANES), lambda i: (i, 0)),
        out_shape=jax.ShapeDtypeStruct((n * PACK_ROWS, LANES), jnp.uint32),
        compiler_params=_cparams("parallel"), name="pack_table")(tables)


def _gather_rows(idx_ref, tab_ref, tt, stage):
    for g in range(N_SLOTS // IDX_GROUP):
        sub = idx_ref.at[0, 0, pl.ds(tt * N_SLOTS + g * IDX_GROUP, IDX_GROUP)]
        for j in range(IDX_GROUP):
            row0 = pl.multiple_of(sub[j], PACK_ROWS)
            stage[pl.ds(PACK_ROWS * (g * IDX_GROUP + j), PACK_ROWS), :] = tab_ref[pl.ds(row0, PACK_ROWS), :]


def _staged_chunks(stage):
    for s in range(PACK_ROWS):
        lo, hi = _unpack(stage[pl.ds(s, N_SLOTS, stride=PACK_ROWS), :])
        yield s, lo, hi


def _token_pipeline(n_tokens, gather, compute, stages):
    k = len(stages)
    last = n_tokens - 1
    gather(0, stages[0])

    def trip(i, carry):
        t0 = k * i
        for u in range(k):
            nxt = t0 + u + 1
            gather(nxt if u < k - 1 else jnp.minimum(nxt, last), stages[(u + 1) % k])
            compute(t0 + u, stages[u])
        return carry

    lax.fori_loop(0, n_tokens // k, trip, 0)


def _expert_act_kernel(idx_ref, hn_ref, gate_ref, tab_ref, w_out, *stages):
    def compute(tt, stage):
        x = hn_ref[tt]
        acc = jnp.zeros((N_SLOTS, LANES), F32)
        for s, lo, hi in _staged_chunks(stage):
            acc = acc + lo * x[s:s + 1] + hi * x[PACK_ROWS + s:PACK_ROWS + s + 1]
        act = jnp.sum(acc.T, axis=0, keepdims=True)
        gelu = 0.5 * act * (1.0 + lax.erf(act * (2.0 ** -0.5)))
        w_out[tt] = gate_ref[tt] * gelu

    _token_pipeline(hn_ref.shape[0], functools.partial(_gather_rows, idx_ref, tab_ref), compute, stages)


N_STAGES = 4


def _stage_scratch():
    return pltpu.VMEM((N_SLOTS * PACK_ROWS, LANES), jnp.uint32)


def _expert_act(idx3, hn3, gate3, table):
    T = hn3.shape[0]
    tb = idx3.shape[2] // N_SLOTS
    return pl.pallas_call(
        _expert_act_kernel, grid=(T // tb,),
        in_specs=[pl.BlockSpec((1, 1, tb * N_SLOTS), lambda i: (i, 0, 0), memory_space=pltpu.SMEM),
                  pl.BlockSpec((tb, SUBLANES, LANES), lambda i: (i, 0, 0)),
                  pl.BlockSpec((tb, 1, N_SLOTS), lambda i: (i, 0, 0)),
                  pl.BlockSpec(memory_space=pltpu.VMEM)],
        out_specs=pl.BlockSpec((tb, 1, N_SLOTS), lambda i: (i, 0, 0)),
        out_shape=jax.ShapeDtypeStruct((T, 1, N_SLOTS), F32),
        scratch_shapes=[_stage_scratch()] * N_STAGES,
        compiler_params=_cparams("arbitrary"), name="expert_act")(idx3, hn3, gate3, table)


def _expert_sum_kernel(idx_ref, w_ref, tab_ref, o_ref, *stages):
    def compute(tt, stage):
        w_col = jnp.broadcast_to(w_ref[tt], (LANES, N_SLOTS)).T
        lo_rows, hi_rows = [], []
        for _, lo, hi in _staged_chunks(stage):
            lo_rows.append(jnp.sum(w_col * lo, axis=0, keepdims=True))
            hi_rows.append(jnp.sum(w_col * hi, axis=0, keepdims=True))
        o_ref[tt] = jnp.concatenate(lo_rows + hi_rows, axis=0)

    _token_pipeline(o_ref.shape[0], functools.partial(_gather_rows, idx_ref, tab_ref), compute, stages)


def _expert_sum(idx3, w3, table):
    T = w3.shape[0]
    tb = idx3.shape[2] // N_SLOTS
    return pl.pallas_call(
        _expert_sum_kernel, grid=(T // tb,),
        in_specs=[pl.BlockSpec((1, 1, tb * N_SLOTS), lambda i: (i, 0, 0), memory_space=pltpu.SMEM),
                  pl.BlockSpec((tb, 1, N_SLOTS), lambda i: (i, 0, 0)),
                  pl.BlockSpec(memory_space=pltpu.VMEM)],
        out_specs=pl.BlockSpec((tb, SUBLANES, LANES), lambda i: (i, 0, 0)),
        out_shape=jax.ShapeDtypeStruct((T, SUBLANES, LANES), F32),
        scratch_shapes=[_stage_scratch()] * N_STAGES,
        compiler_params=_cparams("arbitrary"), name="expert_sum")(idx3, w3, table)


def _final_kernel(h_ref, p_ref, g_ref, o_ref):
    o_ref[...] = _rms(h_ref[...] + _load_rows3(p_ref), g_ref[...])


def _final_norm(h, p, g):
    T = h.shape[0]
    tm = min(ROW_TILE, T)
    row = pl.BlockSpec((tm, D_MODEL), lambda i: (i, 0))
    return pl.pallas_call(
        _final_kernel, grid=(T // tm,),
        in_specs=[row, pl.BlockSpec((tm, SUBLANES, LANES), lambda i: (i, 0, 0)), pl.BlockSpec((1, D_MODEL), lambda i: (0, 0))],
        out_specs=row, out_shape=jax.ShapeDtypeStruct((T, D_MODEL), F32),
        compiler_params=_cparams("parallel"), name="final_norm")(h, p, g)


def _rope_tables(positions):
    half = ROPE_DIM // 2
    inv_freq = ROPE_THETA ** (-jnp.arange(0, ROPE_DIM, 2, dtype=F32) / ROPE_DIM)
    ang = positions.reshape(-1).astype(F32)[:, None] * inv_freq
    cos, sin = jnp.cos(ang), jnp.sin(ang)
    T = cos.shape[0]
    pad = HEAD_DIM - ROPE_DIM
    head = lambda first, second, fill: jnp.concatenate(
        [first, second, jnp.full((T, pad), fill, F32)], axis=1)
    zeros = jnp.zeros_like(sin)
    reps = LANES // HEAD_DIM
    return tuple(jnp.tile(head(*parts), (1, reps))
                 for parts in ((cos, cos, 1.0), (-sin, zeros, 0.0), (zeros, sin, 0.0)))


def _reorder_in_proj(w):
    a0 = ATTN_WIDTH
    a1, a2 = a0 + KV_WIDTH, a0 + 2 * KV_WIDTH
    a3 = a2 + 3 * HYENA_WIDTH
    q, k, v, hy, gates = w[:, :a0], w[:, a0:a1], w[:, a1:a2], w[:, a2:a3], w[:, a3:]
    return jnp.concatenate([gates, hy, q, k, v], axis=1).astype(BF16)


def kernel(x, positions, norm_mix, w_in, conv_w, conv_b, filt_w1, filt_b1, filt_freq1, filt_w2, filt_b2, filt_freq2,
           filt_w3, filt_b3, hyena_skip, attn_sink, w_attn_branch, w_hyena_branch, w_out, norm_ffn, w_query,
           sub_keys, expert_u, expert_v, norm_final):
    B, S, D = x.shape
    assert D == D_MODEL and S % BLOCK == 0
    T = B * S
    depth = w_in.shape[0]
    tb = min(GATHER_TOKENS, T)
    cos_t, s1_t, s2_t = _rope_tables(positions)
    fmat, gmat = _dft_matrices(S)
    z_emb, decay = _filter_inputs(S)

    h = x.reshape(T, D)
    peer = None
    for l in range(depth):
        res = _in_proj(h, peer, norm_mix[l][None], _reorder_in_proj(w_in[l]), cos_t, s1_t, s2_t)
        xcur, proj = (h, res) if peer is None else res
        attn = _attention(proj, attn_sink[l], B, S)
        x0, z, zb = _hyena_prep(proj, conv_w[l], conv_b[l][None], B, S)
        kre, kim = _hyena_filter_spectrum(S, z_emb, decay, fmat, filt_w1[l], filt_b1[l], filt_freq1[l], filt_w2[l],
                                          filt_b2[l], filt_freq2[l], filt_w3[l], filt_b3[l])
        ylong = _long_conv(zb, fmat, gmat, kre, kim, B, S)
        h, hn, q = _merge(xcur, attn, x0, z, ylong, proj, hyena_skip[l][None], w_attn_branch[l].astype(BF16),
                          w_hyena_branch[l].astype(BF16), w_out[l].astype(BF16), norm_ffn[l][None],
                          w_query[l].astype(BF16))
        idx, gates = _peer_topk(q, sub_keys[l].astype(BF16))
        idx3 = idx.reshape(T // tb, 1, tb * N_SLOTS)
        wts = _expert_act(idx3, hn.reshape(T, SUBLANES, LANES), gates.reshape(T, 1, N_SLOTS),
                          _pack_table(expert_u, l))
        peer = _expert_sum(idx3, wts, _pack_table(expert_v, l))
    return _final_norm(h, peer, norm_final[None]).reshape(B, S, D)
```

```python
import functools
import math

import jax
import jax.numpy as jnp
from jax import lax
from jax.experimental import pallas as pl
from jax.experimental.pallas import tpu as pltpu

F32 = jnp.float32
BF16 = jnp.bfloat16

D_MODEL = 1024
N_Q_HEADS = 8
N_KV_HEADS = 2
HEAD_DIM = 64
GQA_GROUP = N_Q_HEADS // N_KV_HEADS
ATTN_WIDTH = N_Q_HEADS * HEAD_DIM
KV_WIDTH = N_KV_HEADS * HEAD_DIM
WINDOW = 128
BLOCK = 128
ROPE_DIM = HEAD_DIM // 4
ROPE_THETA = 500000.0
HYENA_WIDTH = 512
FILTER_EMB_DIM = 33
FILTER_HIDDEN = 64
DECAY_TARGET = 1e-2
MAX_DECAY = math.log(DECAY_TARGET) / 0.3
MIN_DECAY = math.log(DECAY_TARGET) / 1.5
N_KEYS = 128
N_EXPERTS = N_KEYS * N_KEYS
PEER_HEADS = 8
PEER_TOPK = 16
HALF_QUERY = 128
N_SLOTS = PEER_HEADS * PEER_TOPK
EPS = 1e-6

LANES = 128
SUBLANES = 8
VMEM_LIMIT_BYTES = 56 * 1024 * 1024

GATE_OFF = 0
HY_OFF = 2 * D_MODEL
Q_OFF = HY_OFF + 3 * HYENA_WIDTH
K_OFF = Q_OFF + ATTN_WIDTH
V_OFF = K_OFF + KV_WIDTH
IN_WIDTH = V_OFF + KV_WIDTH
ROPE_WIDTH = ATTN_WIDTH + KV_WIDTH

ROW_TILE = 256
IN_PROJ_TILE = 512
FREQ_TILE = 512
TOPK_TILE = 256
GATHER_TOKENS = 64
HALF_D = D_MODEL // 2
PACK_ROWS = HALF_D // LANES


def _cparams(*sem):
    return pltpu.CompilerParams(dimension_semantics=sem, vmem_limit_bytes=VMEM_LIMIT_BYTES)


def _load_rows3(ref):
    return jnp.concatenate([ref[:, s, :] for s in range(SUBLANES)], axis=1)


def _rms(x, g):
    r = lax.rsqrt(jnp.mean(x * x, axis=-1, keepdims=True) + EPS)
    return x * r * g


def _in_proj_kernel(*refs, has_add):
    if has_add:
        h_ref, p_ref, g_ref, w_ref, c_ref, s1_ref, s2_ref, x_out, proj_out = refs
        x = h_ref[...] + _load_rows3(p_ref)
        x_out[...] = x
    else:
        h_ref, g_ref, w_ref, c_ref, s1_ref, s2_ref, proj_out = refs
        x = h_ref[...]
    xn = _rms(x, g_ref[...]).astype(BF16)
    proj_out[:, :Q_OFF] = jnp.dot(xn, w_ref[:, :Q_OFF], preferred_element_type=F32)
    qk = jnp.dot(xn, w_ref[:, Q_OFF:V_OFF], preferred_element_type=F32)
    reps = ROPE_WIDTH // LANES
    cos = jnp.concatenate([c_ref[...]] * reps, axis=1)
    s1 = jnp.concatenate([s1_ref[...]] * reps, axis=1)
    s2 = jnp.concatenate([s2_ref[...]] * reps, axis=1)
    half = ROPE_DIM // 2
    rot = qk * cos + pltpu.roll(qk, ROPE_WIDTH - half, axis=1) * s1 + pltpu.roll(qk, half, axis=1) * s2
    proj_out[:, Q_OFF:V_OFF] = rot
    proj_out[:, V_OFF:] = jnp.dot(xn, w_ref[:, V_OFF:], preferred_element_type=F32)


def _in_proj(h, p, g, w, cos_t, s1_t, s2_t):
    T = h.shape[0]
    tm = min(IN_PROJ_TILE, T)
    has_add = p is not None
    row = lambda i: (i, 0)
    fixed = lambda i: (0, 0)
    x_spec = pl.BlockSpec((tm, D_MODEL), row)
    tab_spec = pl.BlockSpec((tm, LANES), row)
    in_specs = [x_spec] + ([pl.BlockSpec((tm, SUBLANES, LANES), lambda i: (i, 0, 0))] if has_add else []) + [
        pl.BlockSpec((1, D_MODEL), fixed), pl.BlockSpec((D_MODEL, IN_WIDTH), fixed), tab_spec, tab_spec, tab_spec]
    proj_shape = jax.ShapeDtypeStruct((T, IN_WIDTH), F32)
    proj_spec = pl.BlockSpec((tm, IN_WIDTH), row)
    if has_add:
        out_shape, out_specs = (jax.ShapeDtypeStruct((T, D_MODEL), F32), proj_shape), (x_spec, proj_spec)
        args = (h, p, g, w, cos_t, s1_t, s2_t)
    else:
        out_shape, out_specs = proj_shape, proj_spec
        args = (h, g, w, cos_t, s1_t, s2_t)
    return pl.pallas_call(
        functools.partial(_in_proj_kernel, has_add=has_add), grid=(T // tm,), in_specs=in_specs,
        out_specs=out_specs, out_shape=out_shape, compiler_params=_cparams("parallel"), name="in_proj")(*args)


ATTN_Q_BLOCKS = 2


def _attn_kernel(sink_ref, q_ref, kp_ref, ko_ref, kn_ref, o_ref, *, n_steps):
    i = pl.program_id(1)
    kv = jnp.concatenate([kp_ref[...], ko_ref[...], kn_ref[...]], axis=0)
    c = lax.broadcasted_iota(jnp.int32, (3 * BLOCK, BLOCK), 0)
    r = lax.broadcasted_iota(jnp.int32, (3 * BLOCK, BLOCK), 1)
    d = c - r
    band = (d >= BLOCK - WINDOW) & (d <= BLOCK + WINDOW)
    scale = HEAD_DIM ** -0.5
    for u in range(ATTN_Q_BLOCKS):
        valid = band
        if u == 0:
            valid = valid & ((c >= BLOCK) | (i > 0))
        if u == ATTN_Q_BLOCKS - 1:
            valid = valid & ((c < 2 * BLOCK) | (i < n_steps - 1))
        valid = jnp.concatenate([valid] * GQA_GROUP, axis=1)
        q = q_ref[u * BLOCK:(u + 1) * BLOCK, :] * scale
        kv_u = kv[u * BLOCK:(u + 3) * BLOCK]
        v_t = kv_u[:, KV_WIDTH:].T.astype(BF16)
        for g in range(N_KV_HEADS):
            k = kv_u[:, g * HEAD_DIM:(g + 1) * HEAD_DIM].astype(BF16)
            heads = [g * GQA_GROUP + hh for hh in range(GQA_GROUP)]
            qs = jnp.concatenate([q[:, h * HEAD_DIM:(h + 1) * HEAD_DIM] for h in heads], axis=0).astype(BF16)
            s = lax.dot_general(k, qs, (((1,), (1,)), ((), ())), preferred_element_type=F32)
            s = jnp.where(valid, s, -1e30)
            sink = jnp.concatenate([jnp.full((1, BLOCK), sink_ref[h], F32) for h in heads], axis=1)
            m = jnp.maximum(jnp.max(s, axis=0, keepdims=True), sink)
            p = jnp.exp(s - m)
            denom = jnp.sum(p, axis=0, keepdims=True) + jnp.exp(sink - m)
            o_t = jnp.dot(v_t[g * HEAD_DIM:(g + 1) * HEAD_DIM], p.astype(BF16), preferred_element_type=F32)
            o_t = o_t * (1.0 / denom)
            for hh in range(0, GQA_GROUP, 2):
                pair = jnp.concatenate([o_t[:, hh * BLOCK:(hh + 1) * BLOCK],
                                        o_t[:, (hh + 1) * BLOCK:(hh + 2) * BLOCK]], axis=0).T
                col = heads[hh] * HEAD_DIM
                o_ref[u * BLOCK:(u + 1) * BLOCK, col:col + 2 * HEAD_DIM] = pair.astype(o_ref.dtype)


def _attention(proj, sink, B, S):
    nb = S // BLOCK
    nq = ATTN_Q_BLOCKS
    assert nb % nq == 0
    ns = nb // nq
    qcol = Q_OFF // ATTN_WIDTH
    kvcol = K_OFF // (2 * KV_WIDTH)
    edge_spec = lambda f: pl.BlockSpec((BLOCK, 2 * KV_WIDTH), f)
    return pl.pallas_call(
        functools.partial(_attn_kernel, n_steps=ns), grid=(B, ns),
        in_specs=[pl.BlockSpec(memory_space=pltpu.SMEM),
                  pl.BlockSpec((nq * BLOCK, ATTN_WIDTH), lambda b, i: (b * ns + i, qcol)),
                  edge_spec(lambda b, i: (b * nb + jnp.maximum(i * nq - 1, 0), kvcol)),
                  pl.BlockSpec((nq * BLOCK, 2 * KV_WIDTH), lambda b, i: (b * ns + i, kvcol)),
                  edge_spec(lambda b, i: (b * nb + jnp.minimum((i + 1) * nq, nb - 1), kvcol))],
        out_specs=pl.BlockSpec((nq * BLOCK, ATTN_WIDTH), lambda b, i: (b * ns + i, 0)),
        out_shape=jax.ShapeDtypeStruct((B * S, ATTN_WIDTH), BF16),
        compiler_params=_cparams("parallel", "parallel"), name="window_attn")(sink, proj, proj, proj, proj)


HY_COLS = 256


def _short_conv(u, w, b):
    L = u.shape[0]
    row = lax.broadcasted_iota(jnp.int32, u.shape, 0)
    prev = jnp.where(row == 0, 0.0, pltpu.roll(u, 1, axis=0))
    nxt = jnp.where(row == L - 1, 0.0, pltpu.roll(u, L - 1, axis=0))
    return w[0:1] * prev + w[1:2] * u + w[2:3] * nxt + b


def _hyena_prep_kernel(u0, u1, u2, w0, w1, w2, b0, b1, b2, x0_out, z_out, zb_out):
    x0_out[...] = _short_conv(u0[...], w0[...], b0[...])
    z = _short_conv(u1[...], w1[...], b1[...]) * _short_conv(u2[...], w2[...], b2[...])
    z_out[...] = z
    zb_out[...] = z.astype(BF16)


def _hyena_prep(proj, conv_w, conv_b, B, L):
    nc = HYENA_WIDTH // HY_COLS
    base = HY_OFF // HY_COLS
    u_spec = lambda part: pl.BlockSpec((L, HY_COLS), lambda b, c: (b, base + part * nc + c))
    w_spec = lambda part: pl.BlockSpec((3, HY_COLS), lambda b, c: (0, part * nc + c))
    b_spec = lambda part: pl.BlockSpec((1, HY_COLS), lambda b, c: (0, part * nc + c))
    o_spec = pl.BlockSpec((L, HY_COLS), lambda b, c: (b, c))
    shp = lambda dt: jax.ShapeDtypeStruct((B * L, HYENA_WIDTH), dt)
    return pl.pallas_call(
        _hyena_prep_kernel, grid=(B, nc),
        in_specs=[u_spec(0), u_spec(1), u_spec(2), w_spec(0), w_spec(1), w_spec(2), b_spec(0), b_spec(1), b_spec(2)],
        out_specs=(o_spec, o_spec, o_spec), out_shape=(shp(F32), shp(F32), shp(BF16)),
        compiler_params=_cparams("parallel", "parallel"), name="hyena_prep",
    )(proj, proj, proj, conv_w, conv_w, conv_w, conv_b, conv_b, conv_b)


def _filter_mlp_kernel(z_ref, w1, b1, f1, w2, b2, f2, w3, b3, dec_ref, o_ref):
    h = jnp.sin(f1[...] * (jnp.dot(z_ref[...], w1[...], preferred_element_type=F32) + b1[...]))
    h = jnp.sin(f2[...] * (jnp.dot(h, w2[...], preferred_element_type=F32) + b2[...]))
    h = jnp.dot(h, w3[...], preferred_element_type=F32) + b3[...]
    dec = dec_ref[...]
    o_ref[...] = h * jnp.concatenate([dec, dec], axis=1)


def _filter_mlp(z, w1, b1, f1, w2, b2, f2, w3, b3, decay):
    L = z.shape[0]
    tl = min(FREQ_TILE, L)
    fixed = lambda a: pl.BlockSpec(a.shape, lambda i: (0, 0))
    return pl.pallas_call(
        _filter_mlp_kernel, grid=(L // tl,),
        in_specs=[pl.BlockSpec((tl, LANES), lambda i: (i, 0))] + [fixed(a) for a in (w1, b1, f1, w2, b2, f2, w3, b3)]
        + [pl.BlockSpec((tl, HYENA_WIDTH), lambda i: (i, 0))],
        out_specs=pl.BlockSpec((tl, 2 * HYENA_WIDTH), lambda i: (i, 0)),
        out_shape=jax.ShapeDtypeStruct((L, 2 * HYENA_WIDTH), F32),
        compiler_params=_cparams("parallel"), name="filter_mlp")(z, w1, b1, f1, w2, b2, f2, w3, b3, decay)


def _split_bf16(a):
    hi = a.astype(BF16)
    return hi, (a - hi.astype(F32)).astype(BF16)


def _filter_dft_kernel(fre, fim, k0_ref, k1_ref, o_re, o_im, *, n_fft):
    ft = fre.shape[0]
    k = pl.program_id(0) * ft + lax.broadcasted_iota(jnp.int32, (ft, 1), 0)
    sign_im = jnp.where(k == 0, 1.0, -1.0)
    amp = jnp.where(k == 0, 1.0 / n_fft, 2.0 / n_fft)
    k0h, k0l = _split_bf16(k0_ref[...])
    k1h, k1l = _split_bf16(k1_ref[...])

    def part(f, sign):
        d = lambda a: jnp.dot(f, a, preferred_element_type=F32)
        return amp * ((d(k0h) + d(k0l)) + sign * (d(k1h) + d(k1l)))

    o_re[...] = part(fre[...], 1.0)
    o_im[...] = part(fim[...], sign_im)


def _filter_dft(fmat, k0, k1):
    L = k0.shape[0]
    ft = min(FREQ_TILE, L)
    nf = L // ft
    half = pl.BlockSpec((L, HYENA_WIDTH), lambda f: (0, 0))
    o_spec = pl.BlockSpec((ft, HYENA_WIDTH), lambda f: (f, 0))
    shp = jax.ShapeDtypeStruct((L, HYENA_WIDTH), F32)
    return pl.pallas_call(
        functools.partial(_filter_dft_kernel, n_fft=2 * L), grid=(nf,),
        in_specs=[pl.BlockSpec((ft, L), lambda f: (f, 0)), pl.BlockSpec((ft, L), lambda f: (nf + f, 0)), half, half],
        out_specs=(o_spec, o_spec), out_shape=(shp, shp),
        compiler_params=_cparams("parallel"), name="filter_dft")(fmat, fmat, k0, k1)


def _long_conv_kernel(z_ref, fre, fim, gre, gim, kre_ref, kim_ref, y_ref):
    f = pl.program_id(1)
    z = z_ref[...]
    zre = jnp.dot(fre[...], z, preferred_element_type=F32)
    zim = jnp.dot(fim[...], z, preferred_element_type=F32)
    kre, kim = kre_ref[...], kim_ref[...]
    a, b, c, d = zre * kre, zim * kim, zre * kim, zim * kre
    real_row = (lax.broadcasted_iota(jnp.int32, zre.shape, 0) == 0) & (f == 0)
    pre = jnp.where(real_row, a, a - b).astype(BF16)
    pim = jnp.where(real_row, b, c + d).astype(BF16)
    y = jnp.dot(gre[...], pre, preferred_element_type=F32) + jnp.dot(gim[...], pim, preferred_element_type=F32)

    @pl.when(f == 0)
    def _():
        y_ref[...] = y

    @pl.when(f > 0)
    def _():
        y_ref[...] += y


def _long_conv(zb, fmat, gmat, kre, kim, B, L):
    ft = min(FREQ_TILE, L)
    nf = L // ft
    k_spec = pl.BlockSpec((ft, HYENA_WIDTH), lambda b, f: (f, 0))
    return pl.pallas_call(
        _long_conv_kernel, grid=(B, nf),
        in_specs=[pl.BlockSpec((L, HYENA_WIDTH), lambda b, f: (b, 0)),
                  pl.BlockSpec((ft, L), lambda b, f: (f, 0)), pl.BlockSpec((ft, L), lambda b, f: (nf + f, 0)),
                  pl.BlockSpec((L, ft), lambda b, f: (0, f)), pl.BlockSpec((L, ft), lambda b, f: (0, nf + f)),
                  k_spec, k_spec],
        out_specs=pl.BlockSpec((L, HYENA_WIDTH), lambda b, f: (b, 0)),
        out_shape=jax.ShapeDtypeStruct((B * L, HYENA_WIDTH), F32),
        compiler_params=_cparams("parallel", "arbitrary"), name="long_conv")(zb, fmat, fmat, gmat, gmat, kre, kim)


def _dft_matrices(L):
    n = 2 * L

    def build(k, t):
        ang = ((k * t) % n).astype(F32) * (2.0 * math.pi / n)
        re = jnp.cos(ang)
        im = jnp.where(k == 0, (1 - 2 * (t & 1)).astype(F32), -jnp.sin(ang))
        return re.astype(BF16), im.astype(BF16)

    col = jnp.arange(L, dtype=jnp.int32)[:, None]
    row = jnp.arange(L, dtype=jnp.int32)[None, :]
    fmat = jnp.concatenate(build(col, row), axis=0)
    return fmat, fmat.T


def _filter_inputs(L):
    t = jnp.linspace(0.0, 1.0, L, dtype=F32)[:, None]
    bands = (FILTER_EMB_DIM - 1) // 2
    w = 2.0 * math.pi * jnp.arange(L, dtype=F32)[:, None] / L
    f = jnp.linspace(1e-4, bands - 1, bands, dtype=F32)[None, :]
    z = jnp.concatenate([t, jnp.cos(f * w), -jnp.sin(f * w)], axis=-1)
    deltas = jnp.linspace(MIN_DECAY, MAX_DECAY, HYENA_WIDTH, dtype=F32)
    decay = jnp.exp(-t * jnp.abs(deltas)[None, :])
    return jnp.pad(z, ((0, 0), (0, LANES - FILTER_EMB_DIM))), decay


def _pad_to(a, rows, cols):
    return jnp.pad(a, ((0, rows - a.shape[0]), (0, cols - a.shape[1])))


def _hyena_filter_spectrum(L, z_emb, decay, fmat, w1, b1, fr1, w2, b2, fr2, w3, b3):
    h = _filter_mlp(z_emb, _pad_to(w1, LANES, LANES), _pad_to(b1[None], 1, LANES), _pad_to(fr1[None], 1, LANES),
                    _pad_to(w2, LANES, LANES), _pad_to(b2[None], 1, LANES), _pad_to(fr2[None], 1, LANES),
                    _pad_to(w3, LANES, 2 * HYENA_WIDTH), b3[None], decay)
    h_f, h_b = h[:, :HYENA_WIDTH], h[:, HYENA_WIDTH:]
    k0 = jnp.concatenate([h_f[:1] + h_b[:1], h_f[1:]], axis=0)
    k1 = jnp.concatenate([jnp.zeros((1, HYENA_WIDTH), F32), h_b[1:]], axis=0)
    return _filter_dft(fmat, k0, k1)


def _merge_kernel(x_ref, a_ref, x0_ref, z_ref, y_ref, g_ref, skip_ref, wab, whb, wout, nf_ref, wq, h_out, hn_out,
                  q_out):
    ya = jnp.dot(a_ref[...], wab[...], preferred_element_type=F32)
    hy = x0_ref[...] * (y_ref[...] + z_ref[...] * skip_ref[...])
    yh = jnp.dot(hy.astype(BF16), whb[...], preferred_element_type=F32)
    g = g_ref[...]
    merged = jax.nn.sigmoid(g[:, :D_MODEL]) * ya + jax.nn.sigmoid(g[:, D_MODEL:]) * yh
    h = x_ref[...] + jnp.dot(merged.astype(BF16), wout[...], preferred_element_type=F32)
    h_out[...] = h
    hn = _rms(h, nf_ref[...])
    hn_out[...] = hn
    q_out[...] = jnp.dot(hn.astype(BF16), wq[...], preferred_element_type=F32).astype(BF16)


def _merge(x, attn, x0, z, y, proj, skip, wab, whb, wout, nf, wq):
    T = x.shape[0]
    tm = min(ROW_TILE, T)
    row = lambda w: pl.BlockSpec((tm, w), lambda i: (i, 0))
    fixed = lambda a: pl.BlockSpec(a.shape, lambda i: (0, 0))
    qw = wq.shape[1]
    return pl.pallas_call(
        _merge_kernel, grid=(T // tm,),
        in_specs=[row(D_MODEL), row(ATTN_WIDTH), row(HYENA_WIDTH), row(HYENA_WIDTH), row(HYENA_WIDTH),
                  row(2 * D_MODEL), fixed(skip), fixed(wab), fixed(whb), fixed(wout), fixed(nf), fixed(wq)],
        out_specs=(row(D_MODEL), row(D_MODEL), row(qw)),
        out_shape=(jax.ShapeDtypeStruct((T, D_MODEL), F32), jax.ShapeDtypeStruct((T, D_MODEL), F32),
                   jax.ShapeDtypeStruct((T, qw), BF16)),
        compiler_params=_cparams("parallel"), name="merge_query")(x, attn, x0, z, y, proj, skip, wab, whb, wout, nf, wq)


NEG_INF = float("-inf")
POS_SENTINEL = 1e9


def _top_rows_paired(s, pos_a, k):
    half = s.shape[0] // 2
    a, b = s[:half], s[half:]
    pos_b = pos_a + float(half)
    b_wins = b > a
    front, back = jnp.maximum(a, b), jnp.minimum(a, b)
    front_pos = jnp.where(b_wins, pos_b, pos_a)
    back_pos = jnp.where(b_wins, pos_a, pos_b)
    vals, sel = [], []
    for _ in range(k):
        m = jnp.max(front, axis=0, keepdims=True)
        p = jnp.min(jnp.where(front == m, front_pos, POS_SENTINEL), axis=0, keepdims=True)
        hit = front_pos == p
        vals.append(m)
        sel.append(p)
        front = jnp.where(hit, back, front)
        front_pos = jnp.where(hit, back_pos, front_pos)
        back = jnp.where(hit, NEG_INF, back)
    return vals, sel


def _top_rows_paired_payload(halves, k):
    (a, pos_a, pay_a), (b, pos_b, pay_b) = halves
    b_first = (b > a) | ((b == a) & (pos_b < pos_a))
    front, back = jnp.maximum(a, b), jnp.minimum(a, b)
    front_pos, back_pos = jnp.where(b_first, pos_b, pos_a), jnp.where(b_first, pos_a, pos_b)
    front_pay, back_pay = jnp.where(b_first, pay_b, pay_a), jnp.where(b_first, pay_a, pay_b)
    vals, picked = [], []
    for _ in range(k):
        m = jnp.max(front, axis=0, keepdims=True)
        p = jnp.min(jnp.where(front == m, front_pos, POS_SENTINEL), axis=0, keepdims=True)
        hit = front_pos == p
        vals.append(m)
        picked.append(jnp.max(jnp.where(hit, front_pay, -1.0), axis=0, keepdims=True))
        front = jnp.where(hit, back, front)
        front_pos = jnp.where(hit, back_pos, front_pos)
        front_pay = jnp.where(hit, back_pay, front_pay)
        back = jnp.where(hit, NEG_INF, back)
    return vals, picked


def _candidate_segments(kk):
    runs = [(a, kk // (a + 1)) for a in range(kk) if kk // (a + 1) > 1]
    pieces = []
    for a, n in runs:
        pieces += [("row", a, b0, min(b0 + SUBLANES, n)) for b0 in range(0, n, SUBLANES)]
    groups, free = [], []
    for piece in sorted(pieces, key=lambda t: t[2] - t[3]):
        n = piece[3] - piece[2]
        for g, room in enumerate(free):
            if room >= n:
                groups[g].append(piece)
                free[g] -= n
                break
        else:
            groups.append([piece])
            free.append(SUBLANES - n)
    a0 = len(runs)
    for g, room in enumerate(free):
        take = min(room, kk - a0)
        if take:
            groups[g].append(("col", a0, a0 + take))
            a0 += take
    while a0 < kk:
        take = min(SUBLANES, kk - a0)
        groups.append([("col", a0, a0 + take)])
        a0 += take
    return [piece for group in groups for piece in group]


def _peer_topk_kernel(q_ref, keys_ref, idx_out, gate_out):
    c = q_ref.shape[0]
    kk = PEER_TOPK
    row_id = lambda n: lax.broadcasted_iota(jnp.int32, (n, c), 0).astype(F32)
    half_pos = row_id(N_KEYS // 2)
    segments = _candidate_segments(kk)
    n_cand = sum(t[-1] - t[-2] for t in segments)
    half_rows = -(-n_cand // (2 * SUBLANES)) * SUBLANES
    halves, rows = ([], []), 0
    for t in segments:
        halves[rows >= half_rows].append(t)
        rows += t[-1] - t[-2]
    pads = [half_rows - sum(t[-1] - t[-2] for t in part) for part in halves]

    def build(part, pad, row_piece, col_piece, fill):
        pieces = [row_piece(*t[1:]) if t[0] == "row" else col_piece(*t[1:]) for t in part]
        return jnp.concatenate(pieces + [jnp.full((pad, c), fill, F32)] * (pad > 0), axis=0)

    cpos = [build(part, pad, lambda a, b0, b1: (a * kk + b0) + row_id(b1 - b0),
                  lambda a0, a1: (a0 + row_id(a1 - a0)) * kk, POS_SENTINEL) for part, pad in zip(halves, pads)]
    idx_rows, gate_rows = [], []
    for h in range(PEER_HEADS):
        tops = []
        for p in range(2):
            qhp = q_ref[:, (2 * h + p) * HALF_QUERY:(2 * h + p + 1) * HALF_QUERY]
            s = lax.dot_general(keys_ref[p], qhp, (((1,), (1,)), ((), ())), preferred_element_type=F32)
            vals, sel = _top_rows_paired(s, half_pos, kk)
            tops.append((jnp.concatenate(vals, axis=0), jnp.concatenate(sel, axis=0)))
        (s0, i0), (s1, i1) = tops
        cand = [build(part, pad, lambda a, b0, b1: s0[a:a + 1] + s1[b0:b1],
                      lambda a0, a1: s0[a0:a1] + s1[0:1], NEG_INF) for part, pad in zip(halves, pads)]
        eidx = [build(part, pad, lambda a, b0, b1: i0[a:a + 1] * N_KEYS + i1[b0:b1],
                      lambda a0, a1: i0[a0:a1] * N_KEYS + i1[0:1], -1.0) for part, pad in zip(halves, pads)]
        best, chosen = _top_rows_paired_payload(list(zip(cand, cpos, eidx)), kk)
        best = jnp.concatenate(best, axis=0)
        e = jnp.exp(best - best[0:1])
        gate_rows.append(e / jnp.sum(e, axis=0, keepdims=True))
        idx_rows.append(jnp.concatenate(chosen, axis=0))
    idx = jnp.concatenate(idx_rows, axis=0)
    gates = jnp.concatenate(gate_rows, axis=0)
    idx_out[...] = (idx * PACK_ROWS).T.astype(jnp.int32)
    gate_out[...] = gates.T


def _peer_topk(q, keys):
    T = q.shape[0]
    c = min(TOPK_TILE, T)
    o_spec = pl.BlockSpec((c, N_SLOTS), lambda i: (i, 0))
    return pl.pallas_call(
        _peer_topk_kernel, grid=(T // c,),
        in_specs=[pl.BlockSpec((c, q.shape[1]), lambda i: (i, 0)), pl.BlockSpec(keys.shape, lambda i: (0, 0, 0))],
        out_specs=(o_spec, o_spec),
        out_shape=(jax.ShapeDtypeStruct((T, N_SLOTS), jnp.int32), jax.ShapeDtypeStruct((T, N_SLOTS), F32)),
        compiler_params=_cparams("parallel"), name="peer_topk")(q, keys)


HI_MASK = 0xFFFF0000
IDX_GROUP = 8


def _unpack(words):
    lo = lax.bitcast_convert_type(words << 16, F32)
    hi = lax.bitcast_convert_type(words & jnp.uint32(HI_MASK), F32)
    return lo, hi


PACK_TILE = 512


def _pack_kernel(t_ref, o_ref):
    t = t_ref[...]
    bf16_bits = lambda a: lax.bitcast_convert_type(a.astype(BF16).astype(F32), jnp.uint32)
    words = (bf16_bits(t[:, :HALF_D]) >> 16) | (bf16_bits(t[:, HALF_D:]) & jnp.uint32(HI_MASK))
    rows = t.shape[0]
    for s in range(PACK_ROWS):
        o_ref[pl.ds(s, rows, stride=PACK_ROWS), :] = words[:, s * LANES:(s + 1) * LANES]


def _pack_table(tables, layer):
    n = tables.shape[1]
    rows = min(PACK_TILE, n)
    return pl.pallas_call(
        _pack_kernel, grid=(n // rows,), in_specs=[pl.BlockSpec((None, rows, D_MODEL), lambda i: (layer, i, 0))],
        out_specs=pl.BlockSpec((rows * PACK_ROWS, LANES), lambda i: (i, 0)),
        out_shape=jax.ShapeDtypeStruct((n * PACK_ROWS, LANES), jnp.uint32),
        compiler_params=_cparams("parallel"), name="pack_table")(tables)


def _gather_rows(idx_ref, tab_ref, tt, stage):
    for g in range(N_SLOTS // IDX_GROUP):
        sub = idx_ref.at[0, 0, pl.ds(tt * N_SLOTS + g * IDX_GROUP, IDX_GROUP)]
        for j in range(IDX_GROUP):
            row0 = pl.multiple_of(sub[j], PACK_ROWS)
            stage[pl.ds(PACK_ROWS * (g * IDX_GROUP + j), PACK_ROWS), :] = tab_ref[pl.ds(row0, PACK_ROWS), :]


def _staged_chunks(stage):
    for s in range(PACK_ROWS):
        lo, hi = _unpack(stage[pl.ds(s, N_SLOTS, stride=PACK_ROWS), :])
        yield s, lo, hi


def _token_pipeline(n_tokens, gather, compute, stages):
    k = len(stages)
    last = n_tokens - 1
    gather(0, stages[0])

    def trip(i, carry):
        t0 = k * i
        for u in range(k):
            nxt = t0 + u + 1
            gather(nxt if u < k - 1 else jnp.minimum(nxt, last), stages[(u + 1) % k])
            compute(t0 + u, stages[u])
        return carry

    lax.fori_loop(0, n_tokens // k, trip, 0)


def _expert_act_kernel(idx_ref, hn_ref, gate_ref, tab_ref, w_out, *stages):
    def compute(tt, stage):
        x = hn_ref[tt]
        acc = jnp.zeros((N_SLOTS, LANES), F32)
        for s, lo, hi in _staged_chunks(stage):
            acc = acc + lo * x[s:s + 1] + hi * x[PACK_ROWS + s:PACK_ROWS + s + 1]
        act = jnp.sum(acc.T, axis=0, keepdims=True)
        gelu = 0.5 * act * (1.0 + lax.erf(act * (2.0 ** -0.5)))
        w_out[tt] = gate_ref[tt] * gelu

    _token_pipeline(hn_ref.shape[0], functools.partial(_gather_rows, idx_ref, tab_ref), compute, stages)


N_STAGES = 4


def _stage_scratch():
    return pltpu.VMEM((N_SLOTS * PACK_ROWS, LANES), jnp.uint32)


def _expert_act(idx3, hn3, gate3, table):
    T = hn3.shape[0]
    tb = idx3.shape[2] // N_SLOTS
    return pl.pallas_call(
        _expert_act_kernel, grid=(T // tb,),
        in_specs=[pl.BlockSpec((1, 1, tb * N_SLOTS), lambda i: (i, 0, 0), memory_space=pltpu.SMEM),
                  pl.BlockSpec((tb, SUBLANES, LANES), lambda i: (i, 0, 0)),
                  pl.BlockSpec((tb, 1, N_SLOTS), lambda i: (i, 0, 0)),
                  pl.BlockSpec(memory_space=pltpu.VMEM)],
        out_specs=pl.BlockSpec((tb, 1, N_SLOTS), lambda i: (i, 0, 0)),
        out_shape=jax.ShapeDtypeStruct((T, 1, N_SLOTS), F32),
        scratch_shapes=[_stage_scratch()] * N_STAGES,
        compiler_params=_cparams("arbitrary"), name="expert_act")(idx3, hn3, gate3, table)


def _expert_sum_kernel(idx_ref, w_ref, tab_ref, o_ref, *stages):
    def compute(tt, stage):
        w_col = jnp.broadcast_to(w_ref[tt], (LANES, N_SLOTS)).T
        lo_rows, hi_rows = [], []
        for _, lo, hi in _staged_chunks(stage):
            lo_rows.append(jnp.sum(w_col * lo, axis=0, keepdims=True))
            hi_rows.append(jnp.sum(w_col * hi, axis=0, keepdims=True))
        o_ref[tt] = jnp.concatenate(lo_rows + hi_rows, axis=0)

    _token_pipeline(o_ref.shape[0], functools.partial(_gather_rows, idx_ref, tab_ref), compute, stages)


def _expert_sum(idx3, w3, table):
    T = w3.shape[0]
    tb = idx3.shape[2] // N_SLOTS
    return pl.pallas_call(
        _expert_sum_kernel, grid=(T // tb,),
        in_specs=[pl.BlockSpec((1, 1, tb * N_SLOTS), lambda i: (i, 0, 0), memory_space=pltpu.SMEM),
                  pl.BlockSpec((tb, 1, N_SLOTS), lambda i: (i, 0, 0)),
                  pl.BlockSpec(memory_space=pltpu.VMEM)],
        out_specs=pl.BlockSpec((tb, SUBLANES, LANES), lambda i: (i, 0, 0)),
        out_shape=jax.ShapeDtypeStruct((T, SUBLANES, LANES), F32),
        scratch_shapes=[_stage_scratch()] * N_STAGES,
        compiler_params=_cparams("arbitrary"), name="expert_sum")(idx3, w3, table)


def _final_kernel(h_ref, p_ref, g_ref, o_ref):
    o_ref[...] = _rms(h_ref[...] + _load_rows3(p_ref), g_ref[...])


def _final_norm(h, p, g):
    T = h.shape[0]
    tm = min(ROW_TILE, T)
    row = pl.BlockSpec((tm, D_MODEL), lambda i: (i, 0))
    return pl.pallas_call(
        _final_kernel, grid=(T // tm,),
        in_specs=[row, pl.BlockSpec((tm, SUBLANES, LANES), lambda i: (i, 0, 0)), pl.BlockSpec((1, D_MODEL), lambda i: (0, 0))],
        out_specs=row, out_shape=jax.ShapeDtypeStruct((T, D_MODEL), F32),
        compiler_params=_cparams("parallel"), name="final_norm")(h, p, g)


def _rope_tables(positions):
    half = ROPE_DIM // 2
    inv_freq = ROPE_THETA ** (-jnp.arange(0, ROPE_DIM, 2, dtype=F32) / ROPE_DIM)
    ang = positions.reshape(-1).astype(F32)[:, None] * inv_freq
    cos, sin = jnp.cos(ang), jnp.sin(ang)
    T = cos.shape[0]
    pad = HEAD_DIM - ROPE_DIM
    head = lambda first, second, fill: jnp.concatenate(
        [first, second, jnp.full((T, pad), fill, F32)], axis=1)
    zeros = jnp.zeros_like(sin)
    reps = LANES // HEAD_DIM
    return tuple(jnp.tile(head(*parts), (1, reps))
                 for parts in ((cos, cos, 1.0), (-sin, zeros, 0.0), (zeros, sin, 0.0)))


def _reorder_in_proj(w):
    a0 = ATTN_WIDTH
    a1, a2 = a0 + KV_WIDTH, a0 + 2 * KV_WIDTH
    a3 = a2 + 3 * HYENA_WIDTH
    q, k, v, hy, gates = w[:, :a0], w[:, a0:a1], w[:, a1:a2], w[:, a2:a3], w[:, a3:]
    return jnp.concatenate([gates, hy, q, k, v], axis=1).astype(BF16)


def kernel(x, positions, norm_mix, w_in, conv_w, conv_b, filt_w1, filt_b1, filt_freq1, filt_w2, filt_b2, filt_freq2,
           filt_w3, filt_b3, hyena_skip, attn_sink, w_attn_branch, w_hyena_branch, w_out, norm_ffn, w_query,
           sub_keys, expert_u, expert_v, norm_final):
    B, S, D = x.shape
    assert D == D_MODEL and S % BLOCK == 0
    T = B * S
    depth = w_in.shape[0]
    tb = min(GATHER_TOKENS, T)
    cos_t, s1_t, s2_t = _rope_tables(positions)
    fmat, gmat = _dft_matrices(S)
    z_emb, decay = _filter_inputs(S)

    h = x.reshape(T, D)
    peer = None
    for l in range(depth):
        res = _in_proj(h, peer, norm_mix[l][None], _reorder_in_proj(w_in[l]), cos_t, s1_t, s2_t)
        xcur, proj = (h, res) if peer is None else res
        attn = _attention(proj, attn_sink[l], B, S)
        x0, z, zb = _hyena_prep(proj, conv_w[l], conv_b[l][None], B, S)
        kre, kim = _hyena_filter_spectrum(S, z_emb, decay, fmat, filt_w1[l], filt_b1[l], filt_freq1[l], filt_w2[l],
                                          filt_b2[l], filt_freq2[l], filt_w3[l], filt_b3[l])
        ylong = _long_conv(zb, fmat, gmat, kre, kim, B, S)
        h, hn, q = _merge(xcur, attn, x0, z, ylong, proj, hyena_skip[l][None], w_attn_branch[l].astype(BF16),
                          w_hyena_branch[l].astype(BF16), w_out[l].astype(BF16), norm_ffn[l][None],
                          w_query[l].astype(BF16))
        idx, gates = _peer_topk(q, sub_keys[l].astype(BF16))
        idx3 = idx.reshape(T // tb, 1, tb * N_SLOTS)
        wts = _expert_act(idx3, hn.reshape(T, SUBLANES, LANES), gates.reshape(T, 1, N_SLOTS),
                          _pack_table(expert_u, l))
        peer = _expert_sum(idx3, wts, _pack_table(expert_v, l))
    return _final_norm(h, peer, norm_final[None]).reshape(B, S, D)
```

```python
import functools
import math

import jax
import jax.numpy as jnp
from jax import lax
from jax.experimental import pallas as pl
from jax.experimental.pallas import tpu as pltpu

F32 = jnp.float32
BF16 = jnp.bfloat16

D_MODEL = 1024
N_Q_HEADS = 8
N_KV_HEADS = 2
HEAD_DIM = 64
GQA_GROUP = N_Q_HEADS // N_KV_HEADS
ATTN_WIDTH = N_Q_HEADS * HEAD_DIM
KV_WIDTH = N_KV_HEADS * HEAD_DIM
WINDOW = 128
BLOCK = 128
ROPE_DIM = HEAD_DIM // 4
ROPE_THETA = 500000.0
HYENA_WIDTH = 512
FILTER_EMB_DIM = 33
FILTER_HIDDEN = 64
DECAY_TARGET = 1e-2
MAX_DECAY = math.log(DECAY_TARGET) / 0.3
MIN_DECAY = math.log(DECAY_TARGET) / 1.5
N_KEYS = 128
N_EXPERTS = N_KEYS * N_KEYS
PEER_HEADS = 8
PEER_TOPK = 16
HALF_QUERY = 128
N_SLOTS = PEER_HEADS * PEER_TOPK
EPS = 1e-6

LANES = 128
SUBLANES = 8
VMEM_LIMIT_BYTES = 56 * 1024 * 1024

GATE_OFF = 0
HY_OFF = 2 * D_MODEL
Q_OFF = HY_OFF + 3 * HYENA_WIDTH
K_OFF = Q_OFF + ATTN_WIDTH
V_OFF = K_OFF + KV_WIDTH
IN_WIDTH = V_OFF + KV_WIDTH
ROPE_WIDTH = ATTN_WIDTH + KV_WIDTH

ROW_TILE = 256
IN_PROJ_TILE = 512
FREQ_TILE = 512
TOPK_TILE = 256
GATHER_TOKENS = 128
HALF_D = D_MODEL // 2
PACK_ROWS = HALF_D // LANES


def _cparams(*sem):
    return pltpu.CompilerParams(dimension_semantics=sem, vmem_limit_bytes=VMEM_LIMIT_BYTES)


def _load_rows3(ref):
    return jnp.concatenate([ref[:, s, :] for s in range(SUBLANES)], axis=1)


def _rms(x, g):
    r = lax.rsqrt(jnp.mean(x * x, axis=-1, keepdims=True) + EPS)
    return x * r * g


def _in_proj_kernel(*refs, has_add):
    if has_add:
        h_ref, p_ref, g_ref, w_ref, c_ref, s1_ref, s2_ref, x_out, proj_out = refs
        x = h_ref[...] + _load_rows3(p_ref)
        x_out[...] = x
    else:
        h_ref, g_ref, w_ref, c_ref, s1_ref, s2_ref, proj_out = refs
        x = h_ref[...]
    xn = _rms(x, g_ref[...]).astype(BF16)
    proj_out[:, :Q_OFF] = jnp.dot(xn, w_ref[:, :Q_OFF], preferred_element_type=F32)
    qk = jnp.dot(xn, w_ref[:, Q_OFF:V_OFF], preferred_element_type=F32)
    reps = ROPE_WIDTH // LANES
    cos = jnp.concatenate([c_ref[...]] * reps, axis=1)
    s1 = jnp.concatenate([s1_ref[...]] * reps, axis=1)
    s2 = jnp.concatenate([s2_ref[...]] * reps, axis=1)
    half = ROPE_DIM // 2
    rot = qk * cos + pltpu.roll(qk, ROPE_WIDTH - half, axis=1) * s1 + pltpu.roll(qk, half, axis=1) * s2
    proj_out[:, Q_OFF:V_OFF] = rot
    proj_out[:, V_OFF:] = jnp.dot(xn, w_ref[:, V_OFF:], preferred_element_type=F32)


def _in_proj(h, p, g, w, cos_t, s1_t, s2_t):
    T = h.shape[0]
    tm = min(IN_PROJ_TILE, T)
    has_add = p is not None
    row = lambda i: (i, 0)
    fixed = lambda i: (0, 0)
    x_spec = pl.BlockSpec((tm, D_MODEL), row)
    tab_spec = pl.BlockSpec((tm, LANES), row)
    in_specs = [x_spec] + ([pl.BlockSpec((tm, SUBLANES, LANES), lambda i: (i, 0, 0))] if has_add else []) + [
        pl.BlockSpec((1, D_MODEL), fixed), pl.BlockSpec((D_MODEL, IN_WIDTH), fixed), tab_spec, tab_spec, tab_spec]
    proj_shape = jax.ShapeDtypeStruct((T, IN_WIDTH), F32)
    proj_spec = pl.BlockSpec((tm, IN_WIDTH), row)
    if has_add:
        out_shape, out_specs = (jax.ShapeDtypeStruct((T, D_MODEL), F32), proj_shape), (x_spec, proj_spec)
        args = (h, p, g, w, cos_t, s1_t, s2_t)
    else:
        out_shape, out_specs = proj_shape, proj_spec
        args = (h, g, w, cos_t, s1_t, s2_t)
    return pl.pallas_call(
        functools.partial(_in_proj_kernel, has_add=has_add), grid=(T // tm,), in_specs=in_specs,
        out_specs=out_specs, out_shape=out_shape, compiler_params=_cparams("parallel"), name="in_proj")(*args)


ATTN_Q_BLOCKS = 2


def _attn_kernel(sink_ref, q_ref, kp_ref, ko_ref, kn_ref, o_ref, *, n_steps):
    i = pl.program_id(1)
    kv = jnp.concatenate([kp_ref[...], ko_ref[...], kn_ref[...]], axis=0)
    c = lax.broadcasted_iota(jnp.int32, (3 * BLOCK, BLOCK), 0)
    r = lax.broadcasted_iota(jnp.int32, (3 * BLOCK, BLOCK), 1)
    d = c - r
    band = (d >= BLOCK - WINDOW) & (d <= BLOCK + WINDOW)
    scale = HEAD_DIM ** -0.5
    for u in range(ATTN_Q_BLOCKS):
        valid = band
        if u == 0:
            valid = valid & ((c >= BLOCK) | (i > 0))
        if u == ATTN_Q_BLOCKS - 1:
            valid = valid & ((c < 2 * BLOCK) | (i < n_steps - 1))
        valid = jnp.concatenate([valid] * GQA_GROUP, axis=1)
        q = q_ref[u * BLOCK:(u + 1) * BLOCK, :] * scale
        kv_u = kv[u * BLOCK:(u + 3) * BLOCK]
        v_t = kv_u[:, KV_WIDTH:].T.astype(BF16)
        for g in range(N_KV_HEADS):
            k = kv_u[:, g * HEAD_DIM:(g + 1) * HEAD_DIM].astype(BF16)
            heads = [g * GQA_GROUP + hh for hh in range(GQA_GROUP)]
            qs = jnp.concatenate([q[:, h * HEAD_DIM:(h + 1) * HEAD_DIM] for h in heads], axis=0).astype(BF16)
            s = lax.dot_general(k, qs, (((1,), (1,)), ((), ())), preferred_element_type=F32)
            s = jnp.where(valid, s, -1e30)
            sink = jnp.concatenate([jnp.full((1, BLOCK), sink_ref[h], F32) for h in heads], axis=1)
            m = jnp.maximum(jnp.max(s, axis=0, keepdims=True), sink)
            p = jnp.exp(s - m)
            denom = jnp.sum(p, axis=0, keepdims=True) + jnp.exp(sink - m)
            o_t = jnp.dot(v_t[g * HEAD_DIM:(g + 1) * HEAD_DIM], p.astype(BF16), preferred_element_type=F32)
            o_t = o_t * (1.0 / denom)
            for hh in range(0, GQA_GROUP, 2):
                pair = jnp.concatenate([o_t[:, hh * BLOCK:(hh + 1) * BLOCK],
                                        o_t[:, (hh + 1) * BLOCK:(hh + 2) * BLOCK]], axis=0).T
                col = heads[hh] * HEAD_DIM
                o_ref[u * BLOCK:(u + 1) * BLOCK, col:col + 2 * HEAD_DIM] = pair.astype(o_ref.dtype)


def _attention(proj, sink, B, S):
    nb = S // BLOCK
    nq = ATTN_Q_BLOCKS
    assert nb % nq == 0
    ns = nb // nq
    qcol = Q_OFF // ATTN_WIDTH
    kvcol = K_OFF // (2 * KV_WIDTH)
    edge_spec = lambda f: pl.BlockSpec((BLOCK, 2 * KV_WIDTH), f)
    return pl.pallas_call(
        functools.partial(_attn_kernel, n_steps=ns), grid=(B, ns),
        in_specs=[pl.BlockSpec(memory_space=pltpu.SMEM),
                  pl.BlockSpec((nq * BLOCK, ATTN_WIDTH), lambda b, i: (b * ns + i, qcol)),
                  edge_spec(lambda b, i: (b * nb + jnp.maximum(i * nq - 1, 0), kvcol)),
                  pl.BlockSpec((nq * BLOCK, 2 * KV_WIDTH), lambda b, i: (b * ns + i, kvcol)),
                  edge_spec(lambda b, i: (b * nb + jnp.minimum((i + 1) * nq, nb - 1), kvcol))],
        out_specs=pl.BlockSpec((nq * BLOCK, ATTN_WIDTH), lambda b, i: (b * ns + i, 0)),
        out_shape=jax.ShapeDtypeStruct((B * S, ATTN_WIDTH), BF16),
        compiler_params=_cparams("parallel", "parallel"), name="window_attn")(sink, proj, proj, proj, proj)


HY_COLS = 256


def _short_conv(u, w, b):
    L = u.shape[0]
    row = lax.broadcasted_iota(jnp.int32, u.shape, 0)
    prev = jnp.where(row == 0, 0.0, pltpu.roll(u, 1, axis=0))
    nxt = jnp.where(row == L - 1, 0.0, pltpu.roll(u, L - 1, axis=0))
    return w[0:1] * prev + w[1:2] * u + w[2:3] * nxt + b


def _hyena_prep_kernel(u0, u1, u2, w0, w1, w2, b0, b1, b2, x0_out, z_out, zb_out):
    x0_out[...] = _short_conv(u0[...], w0[...], b0[...])
    z = _short_conv(u1[...], w1[...], b1[...]) * _short_conv(u2[...], w2[...], b2[...])
    z_out[...] = z
    zb_out[...] = z.astype(BF16)


def _hyena_prep(proj, conv_w, conv_b, B, L):
    nc = HYENA_WIDTH // HY_COLS
    base = HY_OFF // HY_COLS
    u_spec = lambda part: pl.BlockSpec((L, HY_COLS), lambda b, c: (b, base + part * nc + c))
    w_spec = lambda part: pl.BlockSpec((3, HY_COLS), lambda b, c: (0, part * nc + c))
    b_spec = lambda part: pl.BlockSpec((1, HY_COLS), lambda b, c: (0, part * nc + c))
    o_spec = pl.BlockSpec((L, HY_COLS), lambda b, c: (b, c))
    shp = lambda dt: jax.ShapeDtypeStruct((B * L, HYENA_WIDTH), dt)
    return pl.pallas_call(
        _hyena_prep_kernel, grid=(B, nc),
        in_specs=[u_spec(0), u_spec(1), u_spec(2), w_spec(0), w_spec(1), w_spec(2), b_spec(0), b_spec(1), b_spec(2)],
        out_specs=(o_spec, o_spec, o_spec), out_shape=(shp(F32), shp(F32), shp(BF16)),
        compiler_params=_cparams("parallel", "parallel"), name="hyena_prep",
    )(proj, proj, proj, conv_w, conv_w, conv_w, conv_b, conv_b, conv_b)


def _filter_mlp_kernel(z_ref, w1, b1, f1, w2, b2, f2, w3, b3, dec_ref, o_ref):
    h = jnp.sin(f1[...] * (jnp.dot(z_ref[...], w1[...], preferred_element_type=F32) + b1[...]))
    h = jnp.sin(f2[...] * (jnp.dot(h, w2[...], preferred_element_type=F32) + b2[...]))
    h = jnp.dot(h, w3[...], preferred_element_type=F32) + b3[...]
    dec = dec_ref[...]
    o_ref[...] = h * jnp.concatenate([dec, dec], axis=1)


def _filter_mlp(z, w1, b1, f1, w2, b2, f2, w3, b3, decay):
    L = z.shape[0]
    tl = min(FREQ_TILE, L)
    fixed = lambda a: pl.BlockSpec(a.shape, lambda i: (0, 0))
    return pl.pallas_call(
        _filter_mlp_kernel, grid=(L // tl,),
        in_specs=[pl.BlockSpec((tl, LANES), lambda i: (i, 0))] + [fixed(a) for a in (w1, b1, f1, w2, b2, f2, w3, b3)]
        + [pl.BlockSpec((tl, HYENA_WIDTH), lambda i: (i, 0))],
        out_specs=pl.BlockSpec((tl, 2 * HYENA_WIDTH), lambda i: (i, 0)),
        out_shape=jax.ShapeDtypeStruct((L, 2 * HYENA_WIDTH), F32),
        compiler_params=_cparams("parallel"), name="filter_mlp")(z, w1, b1, f1, w2, b2, f2, w3, b3, decay)


def _split_bf16(a):
    hi = a.astype(BF16)
    return hi, (a - hi.astype(F32)).astype(BF16)


def _filter_dft_kernel(fre, fim, k0_ref, k1_ref, o_re, o_im, *, n_fft):
    ft = fre.shape[0]
    k = pl.program_id(0) * ft + lax.broadcasted_iota(jnp.int32, (ft, 1), 0)
    sign_im = jnp.where(k == 0, 1.0, -1.0)
    amp = jnp.where(k == 0, 1.0 / n_fft, 2.0 / n_fft)
    k0h, k0l = _split_bf16(k0_ref[...])
    k1h, k1l = _split_bf16(k1_ref[...])

    def part(f, sign):
        d = lambda a: jnp.dot(f, a, preferred_element_type=F32)
        return amp * ((d(k0h) + d(k0l)) + sign * (d(k1h) + d(k1l)))

    o_re[...] = part(fre[...], 1.0)
    o_im[...] = part(fim[...], sign_im)


def _filter_dft(fmat, k0, k1):
    L = k0.shape[0]
    ft = min(FREQ_TILE, L)
    nf = L // ft
    half = pl.BlockSpec((L, HYENA_WIDTH), lambda f: (0, 0))
    o_spec = pl.BlockSpec((ft, HYENA_WIDTH), lambda f: (f, 0))
    shp = jax.ShapeDtypeStruct((L, HYENA_WIDTH), F32)
    return pl.pallas_call(
        functools.partial(_filter_dft_kernel, n_fft=2 * L), grid=(nf,),
        in_specs=[pl.BlockSpec((ft, L), lambda f: (f, 0)), pl.BlockSpec((ft, L), lambda f: (nf + f, 0)), half, half],
        out_specs=(o_spec, o_spec), out_shape=(shp, shp),
        compiler_params=_cparams("parallel"), name="filter_dft")(fmat, fmat, k0, k1)


def _long_conv_kernel(z_ref, fre, fim, gre, gim, kre_ref, kim_ref, y_ref):
    f = pl.program_id(1)
    z = z_ref[...]
    zre = jnp.dot(fre[...], z, preferred_element_type=F32)
    zim = jnp.dot(fim[...], z, preferred_element_type=F32)
    kre, kim = kre_ref[...], kim_ref[...]
    a, b, c, d = zre * kre, zim * kim, zre * kim, zim * kre
    real_row = (lax.broadcasted_iota(jnp.int32, zre.shape, 0) == 0) & (f == 0)
    pre = jnp.where(real_row, a, a - b).astype(BF16)
    pim = jnp.where(real_row, b, c + d).astype(BF16)
    y = jnp.dot(gre[...], pre, preferred_element_type=F32) + jnp.dot(gim[...], pim, preferred_element_type=F32)

    @pl.when(f == 0)
    def _():
        y_ref[...] = y

    @pl.when(f > 0)
    def _():
        y_ref[...] += y


def _long_conv(zb, fmat, gmat, kre, kim, B, L):
    ft = min(FREQ_TILE, L)
    nf = L // ft
    k_spec = pl.BlockSpec((ft, HYENA_WIDTH), lambda b, f: (f, 0))
    return pl.pallas_call(
        _long_conv_kernel, grid=(B, nf),
        in_specs=[pl.BlockSpec((L, HYENA_WIDTH), lambda b, f: (b, 0)),
                  pl.BlockSpec((ft, L), lambda b, f: (f, 0)), pl.BlockSpec((ft, L), lambda b, f: (nf + f, 0)),
                  pl.BlockSpec((L, ft), lambda b, f: (0, f)), pl.BlockSpec((L, ft), lambda b, f: (0, nf + f)),
                  k_spec, k_spec],
        out_specs=pl.BlockSpec((L, HYENA_WIDTH), lambda b, f: (b, 0)),
        out_shape=jax.ShapeDtypeStruct((B * L, HYENA_WIDTH), F32),
        compiler_params=_cparams("parallel", "arbitrary"), name="long_conv")(zb, fmat, fmat, gmat, gmat, kre, kim)


def _dft_matrices(L):
    n = 2 * L

    def build(k, t):
        ang = ((k * t) % n).astype(F32) * (2.0 * math.pi / n)
        re = jnp.cos(ang)
        im = jnp.where(k == 0, (1 - 2 * (t & 1)).astype(F32), -jnp.sin(ang))
        return re.astype(BF16), im.astype(BF16)

    col = jnp.arange(L, dtype=jnp.int32)[:, None]
    row = jnp.arange(L, dtype=jnp.int32)[None, :]
    fmat = jnp.concatenate(build(col, row), axis=0)
    return fmat, fmat.T


def _filter_inputs(L):
    t = jnp.linspace(0.0, 1.0, L, dtype=F32)[:, None]
    bands = (FILTER_EMB_DIM - 1) // 2
    w = 2.0 * math.pi * jnp.arange(L, dtype=F32)[:, None] / L
    f = jnp.linspace(1e-4, bands - 1, bands, dtype=F32)[None, :]
    z = jnp.concatenate([t, jnp.cos(f * w), -jnp.sin(f * w)], axis=-1)
    deltas = jnp.linspace(MIN_DECAY, MAX_DECAY, HYENA_WIDTH, dtype=F32)
    decay = jnp.exp(-t * jnp.abs(deltas)[None, :])
    return jnp.pad(z, ((0, 0), (0, LANES - FILTER_EMB_DIM))), decay


def _pad_to(a, rows, cols):
    return jnp.pad(a, ((0, rows - a.shape[0]), (0, cols - a.shape[1])))


def _hyena_filter_spectrum(L, z_emb, decay, fmat, w1, b1, fr1, w2, b2, fr2, w3, b3):
    h = _filter_mlp(z_emb, _pad_to(w1, LANES, LANES), _pad_to(b1[None], 1, LANES), _pad_to(fr1[None], 1, LANES),
                    _pad_to(w2, LANES, LANES), _pad_to(b2[None], 1, LANES), _pad_to(fr2[None], 1, LANES),
                    _pad_to(w3, LANES, 2 * HYENA_WIDTH), b3[None], decay)
    h_f, h_b = h[:, :HYENA_WIDTH], h[:, HYENA_WIDTH:]
    k0 = jnp.concatenate([h_f[:1] + h_b[:1], h_f[1:]], axis=0)
    k1 = jnp.concatenate([jnp.zeros((1, HYENA_WIDTH), F32), h_b[1:]], axis=0)
    return _filter_dft(fmat, k0, k1)


def _merge_kernel(x_ref, a_ref, x0_ref, z_ref, y_ref, g_ref, skip_ref, wab, whb, wout, nf_ref, wq, h_out, hn_out,
                  q_out):
    ya = jnp.dot(a_ref[...], wab[...], preferred_element_type=F32)
    hy = x0_ref[...] * (y_ref[...] + z_ref[...] * skip_ref[...])
    yh = jnp.dot(hy.astype(BF16), whb[...], preferred_element_type=F32)
    g = g_ref[...]
    merged = jax.nn.sigmoid(g[:, :D_MODEL]) * ya + jax.nn.sigmoid(g[:, D_MODEL:]) * yh
    h = x_ref[...] + jnp.dot(merged.astype(BF16), wout[...], preferred_element_type=F32)
    h_out[...] = h
    hn = _rms(h, nf_ref[...])
    hn_out[...] = hn
    q_out[...] = jnp.dot(hn.astype(BF16), wq[...], preferred_element_type=F32).astype(BF16)


def _merge(x, attn, x0, z, y, proj, skip, wab, whb, wout, nf, wq):
    T = x.shape[0]
    tm = min(ROW_TILE, T)
    row = lambda w: pl.BlockSpec((tm, w), lambda i: (i, 0))
    fixed = lambda a: pl.BlockSpec(a.shape, lambda i: (0, 0))
    qw = wq.shape[1]
    return pl.pallas_call(
        _merge_kernel, grid=(T // tm,),
        in_specs=[row(D_MODEL), row(ATTN_WIDTH), row(HYENA_WIDTH), row(HYENA_WIDTH), row(HYENA_WIDTH),
                  row(2 * D_MODEL), fixed(skip), fixed(wab), fixed(whb), fixed(wout), fixed(nf), fixed(wq)],
        out_specs=(row(D_MODEL), row(D_MODEL), row(qw)),
        out_shape=(jax.ShapeDtypeStruct((T, D_MODEL), F32), jax.ShapeDtypeStruct((T, D_MODEL), F32),
                   jax.ShapeDtypeStruct((T, qw), BF16)),
        compiler_params=_cparams("parallel"), name="merge_query")(x, attn, x0, z, y, proj, skip, wab, whb, wout, nf, wq)


NEG_INF = float("-inf")
POS_SENTINEL = 1e9


def _top_rows_paired(s, pos_a, k):
    half = s.shape[0] // 2
    a, b = s[:half], s[half:]
    pos_b = pos_a + float(half)
    b_wins = b > a
    front, back = jnp.maximum(a, b), jnp.minimum(a, b)
    front_pos = jnp.where(b_wins, pos_b, pos_a)
    back_pos = jnp.where(b_wins, pos_a, pos_b)
    vals, sel = [], []
    for _ in range(k):
        m = jnp.max(front, axis=0, keepdims=True)
        p = jnp.min(jnp.where(front == m, front_pos, POS_SENTINEL), axis=0, keepdims=True)
        hit = front_pos == p
        vals.append(m)
        sel.append(p)
        front = jnp.where(hit, back, front)
        front_pos = jnp.where(hit, back_pos, front_pos)
        back = jnp.where(hit, NEG_INF, back)
    return vals, sel


def _top_rows_paired_payload(halves, k):
    (a, pos_a, pay_a), (b, pos_b, pay_b) = halves
    b_first = (b > a) | ((b == a) & (pos_b < pos_a))
    front, back = jnp.maximum(a, b), jnp.minimum(a, b)
    front_pos, back_pos = jnp.where(b_first, pos_b, pos_a), jnp.where(b_first, pos_a, pos_b)
    front_pay, back_pay = jnp.where(b_first, pay_b, pay_a), jnp.where(b_first, pay_a, pay_b)
    vals, picked = [], []
    for _ in range(k):
        m = jnp.max(front, axis=0, keepdims=True)
        p = jnp.min(jnp.where(front == m, front_pos, POS_SENTINEL), axis=0, keepdims=True)
        hit = front_pos == p
        vals.append(m)
        picked.append(jnp.max(jnp.where(hit, front_pay, -1.0), axis=0, keepdims=True))
        front = jnp.where(hit, back, front)
        front_pos = jnp.where(hit, back_pos, front_pos)
        front_pay = jnp.where(hit, back_pay, front_pay)
        back = jnp.where(hit, NEG_INF, back)
    return vals, picked


def _candidate_segments(kk):
    runs = [(a, kk // (a + 1)) for a in range(kk) if kk // (a + 1) > 1]
    pieces = []
    for a, n in runs:
        pieces += [("row", a, b0, min(b0 + SUBLANES, n)) for b0 in range(0, n, SUBLANES)]
    groups, free = [], []
    for piece in sorted(pieces, key=lambda t: t[2] - t[3]):
        n = piece[3] - piece[2]
        for g, room in enumerate(free):
            if room >= n:
                groups[g].append(piece)
                free[g] -= n
                break
        else:
            groups.append([piece])
            free.append(SUBLANES - n)
    a0 = len(runs)
    for g, room in enumerate(free):
        take = min(room, kk - a0)
        if take:
            groups[g].append(("col", a0, a0 + take))
            a0 += take
    while a0 < kk:
        take = min(SUBLANES, kk - a0)
        groups.append([("col", a0, a0 + take)])
        a0 += take
    return [piece for group in groups for piece in group]


def _peer_topk_kernel(q_ref, keys_ref, idx_out, gate_out):
    c = q_ref.shape[0]
    kk = PEER_TOPK
    row_id = lambda n: lax.broadcasted_iota(jnp.int32, (n, c), 0).astype(F32)
    half_pos = row_id(N_KEYS // 2)
    segments = _candidate_segments(kk)
    n_cand = sum(t[-1] - t[-2] for t in segments)
    half_rows = -(-n_cand // (2 * SUBLANES)) * SUBLANES
    halves, rows = ([], []), 0
    for t in segments:
        halves[rows >= half_rows].append(t)
        rows += t[-1] - t[-2]
    pads = [half_rows - sum(t[-1] - t[-2] for t in part) for part in halves]

    def build(part, pad, row_piece, col_piece, fill):
        pieces = [row_piece(*t[1:]) if t[0] == "row" else col_piece(*t[1:]) for t in part]
        return jnp.concatenate(pieces + [jnp.full((pad, c), fill, F32)] * (pad > 0), axis=0)

    cpos = [build(part, pad, lambda a, b0, b1: (a * kk + b0) + row_id(b1 - b0),
                  lambda a0, a1: (a0 + row_id(a1 - a0)) * kk, POS_SENTINEL) for part, pad in zip(halves, pads)]
    idx_rows, gate_rows = [], []
    for h in range(PEER_HEADS):
        tops = []
        for p in range(2):
            qhp = q_ref[:, (2 * h + p) * HALF_QUERY:(2 * h + p + 1) * HALF_QUERY]
            s = lax.dot_general(keys_ref[p], qhp, (((1,), (1,)), ((), ())), preferred_element_type=F32)
            vals, sel = _top_rows_paired(s, half_pos, kk)
            tops.append((jnp.concatenate(vals, axis=0), jnp.concatenate(sel, axis=0)))
        (s0, i0), (s1, i1) = tops
        cand = [build(part, pad, lambda a, b0, b1: s0[a:a + 1] + s1[b0:b1],
                      lambda a0, a1: s0[a0:a1] + s1[0:1], NEG_INF) for part, pad in zip(halves, pads)]
        eidx = [build(part, pad, lambda a, b0, b1: i0[a:a + 1] * N_KEYS + i1[b0:b1],
                      lambda a0, a1: i0[a0:a1] * N_KEYS + i1[0:1], -1.0) for part, pad in zip(halves, pads)]
        best, chosen = _top_rows_paired_payload(list(zip(cand, cpos, eidx)), kk)
        best = jnp.concatenate(best, axis=0)
        e = jnp.exp(best - best[0:1])
        gate_rows.append(e / jnp.sum(e, axis=0, keepdims=True))
        idx_rows.append(jnp.concatenate(chosen, axis=0))
    idx = jnp.concatenate(idx_rows, axis=0)
    gates = jnp.concatenate(gate_rows, axis=0)
    idx_out[...] = (idx * PACK_ROWS).T.astype(jnp.int32)
    gate_out[...] = gates.T


def _peer_topk(q, keys):
    T = q.shape[0]
    c = min(TOPK_TILE, T)
    o_spec = pl.BlockSpec((c, N_SLOTS), lambda i: (i, 0))
    return pl.pallas_call(
        _peer_topk_kernel, grid=(T // c,),
        in_specs=[pl.BlockSpec((c, q.shape[1]), lambda i: (i, 0)), pl.BlockSpec(keys.shape, lambda i: (0, 0, 0))],
        out_specs=(o_spec, o_spec),
        out_shape=(jax.ShapeDtypeStruct((T, N_SLOTS), jnp.int32), jax.ShapeDtypeStruct((T, N_SLOTS), F32)),
        compiler_params=_cparams("parallel"), name="peer_topk")(q, keys)


HI_MASK = 0xFFFF0000
IDX_GROUP = 8


def _unpack(words):
    lo = lax.bitcast_convert_type(words << 16, F32)
    hi = lax.bitcast_convert_type(words & jnp.uint32(HI_MASK), F32)
    return lo, hi


PACK_TILE = 512


def _pack_kernel(t_ref, o_ref):
    t = t_ref[...]
    bf16_bits = lambda a: lax.bitcast_convert_type(a.astype(BF16).astype(F32), jnp.uint32)
    words = (bf16_bits(t[:, :HALF_D]) >> 16) | (bf16_bits(t[:, HALF_D:]) & jnp.uint32(HI_MASK))
    rows = t.shape[0]
    for s in range(PACK_ROWS):
        o_ref[pl.ds(s, rows, stride=PACK_ROWS), :] = words[:, s * LANES:(s + 1) * LANES]


def _pack_table(tables, layer):
    n = tables.shape[1]
    rows = min(PACK_TILE, n)
    return pl.pallas_call(
        _pack_kernel, grid=(n // rows,), in_specs=[pl.BlockSpec((None, rows, D_MODEL), lambda i: (layer, i, 0))],
        out_specs=pl.BlockSpec((rows * PACK_ROWS, LANES), lambda i: (i, 0)),
        out_shape=jax.ShapeDtypeStruct((n * PACK_ROWS, LANES), jnp.uint32),
        compiler_params=_cparams("parallel"), name="pack_table")(tables)


def _gather_rows(idx_ref, tab_ref, tt, stage):
    for g in range(N_SLOTS // IDX_GROUP):
        sub = idx_ref.at[0, 0, pl.ds(tt * N_SLOTS + g * IDX_GROUP, IDX_GROUP)]
        for j in range(IDX_GROUP):
            row0 = pl.multiple_of(sub[j], PACK_ROWS)
            stage[pl.ds(PACK_ROWS * (g * IDX_GROUP + j), PACK_ROWS), :] = tab_ref[pl.ds(row0, PACK_ROWS), :]


def _staged_chunks(stage):
    for s in range(PACK_ROWS):
        lo, hi = _unpack(stage[pl.ds(s, N_SLOTS, stride=PACK_ROWS), :])
        yield s, lo, hi


def _token_pipeline(n_tokens, gather, compute, stages):
    k = len(stages)
    last = n_tokens - 1
    gather(0, stages[0])

    def trip(i, carry):
        t0 = k * i
        for u in range(k):
            nxt = t0 + u + 1
            gather(nxt if u < k - 1 else jnp.minimum(nxt, last), stages[(u + 1) % k])
            compute(t0 + u, stages[u])
        return carry

    lax.fori_loop(0, n_tokens // k, trip, 0)


def _expert_act_kernel(idx_ref, hn_ref, gate_ref, tab_ref, w_out, *stages):
    def compute(tt, stage):
        x = hn_ref[tt]
        acc = jnp.zeros((N_SLOTS, LANES), F32)
        for s, lo, hi in _staged_chunks(stage):
            acc = acc + lo * x[s:s + 1] + hi * x[PACK_ROWS + s:PACK_ROWS + s + 1]
        act = jnp.sum(acc.T, axis=0, keepdims=True)
        gelu = 0.5 * act * (1.0 + lax.erf(act * (2.0 ** -0.5)))
        w_out[tt] = gate_ref[tt] * gelu

    _token_pipeline(hn_ref.shape[0], functools.partial(_gather_rows, idx_ref, tab_ref), compute, stages)


N_STAGES = 4


def _stage_scratch():
    return pltpu.VMEM((N_SLOTS * PACK_ROWS, LANES), jnp.uint32)


def _expert_act(idx3, hn3, gate3, table):
    T = hn3.shape[0]
    tb = idx3.shape[2] // N_SLOTS
    return pl.pallas_call(
        _expert_act_kernel, grid=(T // tb,),
        in_specs=[pl.BlockSpec((1, 1, tb * N_SLOTS), lambda i: (i, 0, 0), memory_space=pltpu.SMEM),
                  pl.BlockSpec((tb, SUBLANES, LANES), lambda i: (i, 0, 0)),
                  pl.BlockSpec((tb, 1, N_SLOTS), lambda i: (i, 0, 0)),
                  pl.BlockSpec(memory_space=pltpu.VMEM)],
        out_specs=pl.BlockSpec((tb, 1, N_SLOTS), lambda i: (i, 0, 0)),
        out_shape=jax.ShapeDtypeStruct((T, 1, N_SLOTS), F32),
        scratch_shapes=[_stage_scratch()] * N_STAGES,
        compiler_params=_cparams("arbitrary"), name="expert_act")(idx3, hn3, gate3, table)


def _expert_sum_kernel(idx_ref, w_ref, tab_ref, o_ref, *stages):
    def compute(tt, stage):
        w_col = jnp.broadcast_to(w_ref[tt], (LANES, N_SLOTS)).T
        lo_rows, hi_rows = [], []
        for _, lo, hi in _staged_chunks(stage):
            lo_rows.append(jnp.sum(w_col * lo, axis=0, keepdims=True))
            hi_rows.append(jnp.sum(w_col * hi, axis=0, keepdims=True))
        o_ref[tt] = jnp.concatenate(lo_rows + hi_rows, axis=0)

    _token_pipeline(o_ref.shape[0], functools.partial(_gather_rows, idx_ref, tab_ref), compute, stages)


def _expert_sum(idx3, w3, table):
    T = w3.shape[0]
    tb = idx3.shape[2] // N_SLOTS
    return pl.pallas_call(
        _expert_sum_kernel, grid=(T // tb,),
        in_specs=[pl.BlockSpec((1, 1, tb * N_SLOTS), lambda i: (i, 0, 0), memory_space=pltpu.SMEM),
                  pl.BlockSpec((tb, 1, N_SLOTS), lambda i: (i, 0, 0)),
                  pl.BlockSpec(memory_space=pltpu.VMEM)],
        out_specs=pl.BlockSpec((tb, SUBLANES, LANES), lambda i: (i, 0, 0)),
        out_shape=jax.ShapeDtypeStruct((T, SUBLANES, LANES), F32),
        scratch_shapes=[_stage_scratch()] * N_STAGES,
        compiler_params=_cparams("arbitrary"), name="expert_sum")(idx3, w3, table)


def _final_kernel(h_ref, p_ref, g_ref, o_ref):
    o_ref[...] = _rms(h_ref[...] + _load_rows3(p_ref), g_ref[...])


def _final_norm(h, p, g):
    T = h.shape[0]
    tm = min(ROW_TILE, T)
    row = pl.BlockSpec((tm, D_MODEL), lambda i: (i, 0))
    return pl.pallas_call(
        _final_kernel, grid=(T // tm,),
        in_specs=[row, pl.BlockSpec((tm, SUBLANES, LANES), lambda i: (i, 0, 0)), pl.BlockSpec((1, D_MODEL), lambda i: (0, 0))],
        out_specs=row, out_shape=jax.ShapeDtypeStruct((T, D_MODEL), F32),
        compiler_params=_cparams("parallel"), name="final_norm")(h, p, g)


def _rope_tables(positions):
    half = ROPE_DIM // 2
    inv_freq = ROPE_THETA ** (-jnp.arange(0, ROPE_DIM, 2, dtype=F32) / ROPE_DIM)
    ang = positions.reshape(-1).astype(F32)[:, None] * inv_freq
    cos, sin = jnp.cos(ang), jnp.sin(ang)
    T = cos.shape[0]
    pad = HEAD_DIM - ROPE_DIM
    head = lambda first, second, fill: jnp.concatenate(
        [first, second, jnp.full((T, pad), fill, F32)], axis=1)
    zeros = jnp.zeros_like(sin)
    reps = LANES // HEAD_DIM
    return tuple(jnp.tile(head(*parts), (1, reps))
                 for parts in ((cos, cos, 1.0), (-sin, zeros, 0.0), (zeros, sin, 0.0)))


def _reorder_in_proj(w):
    a0 = ATTN_WIDTH
    a1, a2 = a0 + KV_WIDTH, a0 + 2 * KV_WIDTH
    a3 = a2 + 3 * HYENA_WIDTH
    q, k, v, hy, gates = w[:, :a0], w[:, a0:a1], w[:, a1:a2], w[:, a2:a3], w[:, a3:]
    return jnp.concatenate([gates, hy, q, k, v], axis=1).astype(BF16)


def kernel(x, positions, norm_mix, w_in, conv_w, conv_b, filt_w1, filt_b1, filt_freq1, filt_w2, filt_b2, filt_freq2,
           filt_w3, filt_b3, hyena_skip, attn_sink, w_attn_branch, w_hyena_branch, w_out, norm_ffn, w_query,
           sub_keys, expert_u, expert_v, norm_final):
    B, S, D = x.shape
    assert D == D_MODEL and S % BLOCK == 0
    T = B * S
    depth = w_in.shape[0]
    tb = min(GATHER_TOKENS, T)
    cos_t, s1_t, s2_t = _rope_tables(positions)
    fmat, gmat = _dft_matrices(S)
    z_emb, decay = _filter_inputs(S)

    h = x.reshape(T, D)
    peer = None
    for l in range(depth):
        res = _in_proj(h, peer, norm_mix[l][None], _reorder_in_proj(w_in[l]), cos_t, s1_t, s2_t)
        xcur, proj = (h, res) if peer is None else res
        attn = _attention(proj, attn_sink[l], B, S)
        x0, z, zb = _hyena_prep(proj, conv_w[l], conv_b[l][None], B, S)
        kre, kim = _hyena_filter_spectrum(S, z_emb, decay, fmat, filt_w1[l], filt_b1[l], filt_freq1[l], filt_w2[l],
                                          filt_b2[l], filt_freq2[l], filt_w3[l], filt_b3[l])
        ylong = _long_conv(zb, fmat, gmat, kre, kim, B, S)
        h, hn, q = _merge(xcur, attn, x0, z, ylong, proj, hyena_skip[l][None], w_attn_branch[l].astype(BF16),
                          w_hyena_branch[l].astype(BF16), w_out[l].astype(BF16), norm_ffn[l][None],
                          w_query[l].astype(BF16))
        idx, gates = _peer_topk(q, sub_keys[l].astype(BF16))
        idx3 = idx.reshape(T // tb, 1, tb * N_SLOTS)
        wts = _expert_act(idx3, hn.reshape(T, SUBLANES, LANES), gates.reshape(T, 1, N_SLOTS),
                          _pack_table(expert_u, l))
        peer = _expert_sum(idx3, wts, _pack_table(expert_v, l))
    return _final_norm(h, peer, norm_final[None]).reshape(B, S, D)
```
